```python
import math
import jax, jax.numpy as jnp
from jax import lax
import numpy as np

D_MODEL = 2048
BATCH = 16
SEQ = 256
DEPTH = 4
DEC_BATCH = 8
DEC_SEQ = 4096
PAST_LEN = 512

GRID_W = 64
N_MIXERS = 4
N_REPEAT = DEPTH // N_MIXERS
NORM_EPS = 1e-6
Q_BLOCK = 128
ROPE_BASE = 10000.0
NEG_INF = -1e30

MLA_HEADS = 16
MLA_Q_RANK = 512
MLA_KV_RANK = 512
MLA_NOPE = 128
MLA_ROPE = 64
MLA_V = 128

GQA_HEADS = 32
GQA_KV_HEADS = 8
GQA_HEAD_DIM = 64
WINDOW = 128
BAND_BLOCK = 128

S5_GROUP_CH = 16
S5_GROUPS = D_MODEL // S5_GROUP_CH
S5_STATE = 64
S5_CHUNK = 16
S5_N_CHUNKS = S5_GROUPS // S5_CHUNK
DT_MIN = 1e-3
DT_MAX = 1e-1

CONV_WIDTH = 31

N_EXPERTS = 32
N_EXPERT_GROUPS = 4
EXPERTS_PER_GROUP = N_EXPERTS // N_EXPERT_GROUPS
TOP_K = 2
D_EXPERT = 512
MOE_BLOCK = 128

kernel_name = 'hybrid_diffusion_prefix_trunk_step'


def rmsnorm(x, g):
    xf = x.astype(jnp.float32)
    y = xf * lax.rsqrt(jnp.mean(xf * xf, axis=-1, keepdims=True) + NORM_EPS)
    return y.astype(x.dtype) * g


def layernorm(x, g, b):
    xf = x.astype(jnp.float32)
    mu = jnp.mean(xf, axis=-1, keepdims=True)
    var = jnp.mean(jnp.square(xf - mu), axis=-1, keepdims=True)
    return ((xf - mu) * lax.rsqrt(var + NORM_EPS)).astype(x.dtype) * g + b


def modulation(cond, w, b):
    m = (jax.nn.silu(cond) @ w + b)[..., None, :]
    return jnp.split(m, 6, axis=-1)


def adaln(x, g, shift, scale):
    return rmsnorm(x, g) * (1 + scale) + shift


def axial_rope(n_tokens, rot_dim, dtype):
    rows = n_tokens // GRID_W
    r = jnp.repeat(jnp.arange(rows), GRID_W).astype(jnp.float32)
    col = jnp.tile(jnp.arange(GRID_W), rows).astype(jnp.float32)
    n_freq = rot_dim // 4
    inv = ROPE_BASE ** (-jnp.arange(n_freq, dtype=jnp.float32) / n_freq)
    ang = jnp.concatenate([r[:, None] * inv, col[:, None] * inv], axis=-1)
    return jnp.cos(ang).astype(dtype), jnp.sin(ang).astype(dtype)


def apply_rope(x, cos, sin):
    half = x.shape[-1] // 2
    x1, x2 = x[..., :half], x[..., half:]
    c = cos[None, :, None, :]
    s = sin[None, :, None, :]
    return jnp.concatenate([x1 * c - x2 * s, x1 * s + x2 * c], axis=-1)


def dense_attention(q, k, v, sink=None):
    b, sq, h, dk = q.shape
    hk = k.shape[2]
    g = h // hk
    dv = v.shape[-1]
    nb = sq // Q_BLOCK
    scale = dk ** -0.5
    qb = q.reshape(b, nb, Q_BLOCK, hk, g, dk).transpose(1, 0, 2, 3, 4, 5)

    def one_block(qi):
        s = jnp.einsum('bqhgd,bkhd->bhgqk', qi, k).astype(jnp.float32) * scale
        if sink is not None:
            sk = jnp.broadcast_to(sink.astype(jnp.float32).reshape(1, hk, g, 1, 1), s.shape[:-1] + (1,))
            p = jax.nn.softmax(jnp.concatenate([s, sk], axis=-1), axis=-1)[..., :-1]
        else:
            p = jax.nn.softmax(s, axis=-1)
        return jnp.einsum('bhgqk,bkhd->bqhgd', p.astype(v.dtype), v)

    o = lax.map(one_block, qb)
    return o.transpose(1, 0, 2, 3, 4, 5).reshape(b, sq, h * dv)


def window_attention(q, k, v, kc, vc, sink):
    b, n, h, d = q.shape
    hk = k.shape[2]
    g = h // hk
    nb = n // BAND_BLOCK
    n_loc = 3 * BAND_BLOCK
    n_ctx = kc.shape[1]
    scale = d ** -0.5
    qb = q.reshape(b, nb, BAND_BLOCK, hk, g, d).transpose(1, 0, 2, 3, 4, 5)
    pad = ((0, 0), (BAND_BLOCK, BAND_BLOCK), (0, 0), (0, 0))
    kp = jnp.pad(k, pad)
    vp = jnp.pad(v, pad)
    sink_l = sink.astype(jnp.float32).reshape(1, hk, g, 1, 1)
    offs_q = jnp.arange(BAND_BLOCK)
    offs_k = jnp.arange(n_loc) - BAND_BLOCK

    def one_block(args):
        qi, bi = args
        start = bi * BAND_BLOCK
        kw = lax.dynamic_slice_in_dim(kp, start, n_loc, axis=1)
        vw = lax.dynamic_slice_in_dim(vp, start, n_loc, axis=1)
        qpos = start + offs_q
        kpos = start + offs_k
        valid = (jnp.abs(qpos[:, None] - kpos[None, :]) <= WINDOW) & (kpos >= 0)[None, :] & (kpos < n)[None, :]
        s_loc = jnp.einsum('bqhgd,bkhd->bhgqk', qi, kw).astype(jnp.float32) * scale
        s_loc = jnp.where(valid, s_loc, NEG_INF)
        s_ctx = jnp.einsum('bqhgd,bkhd->bhgqk', qi, kc).astype(jnp.float32) * scale
        s_snk = jnp.broadcast_to(sink_l, s_loc.shape[:-1] + (1,))
        p = jax.nn.softmax(jnp.concatenate([s_loc, s_ctx, s_snk], axis=-1), axis=-1).astype(v.dtype)
        return (jnp.einsum('bhgqk,bkhd->bqhgd', p[..., :n_loc], vw)
                + jnp.einsum('bhgqk,bkhd->bqhgd', p[..., n_loc:n_loc + n_ctx], vc))

    o = lax.map(one_block, (qb, jnp.arange(nb)))
    return o.transpose(1, 0, 2, 3, 4, 5).reshape(b, n, h * d)


def mla_project(h, wq_a, q_norm, wq_b, wkv_a, kv_norm):
    b, n, _ = h.shape
    q = (rmsnorm(h @ wq_a, q_norm) @ wq_b).reshape(b, n, MLA_HEADS, MLA_NOPE + MLA_ROPE)
    kv_a = h @ wkv_a
    c_kv = rmsnorm(kv_a[..., :MLA_KV_RANK], kv_norm)
    k_rope = kv_a[..., MLA_KV_RANK:]
    return q, c_kv, k_rope


def mla_expand(c_kv, k_rope, wkv_b):
    b, n, _ = c_kv.shape
    kv = (c_kv @ wkv_b).reshape(b, n, MLA_HEADS, MLA_NOPE + MLA_V)
    k_rope_h = jnp.broadcast_to(k_rope[:, :, None, :], (b, n, MLA_HEADS, MLA_ROPE))
    return jnp.concatenate([kv[..., :MLA_NOPE], k_rope_h], axis=-1), kv[..., MLA_NOPE:]


def gqa_project(h, wq, wk, wv):
    b, n, _ = h.shape
    return ((h @ wq).reshape(b, n, GQA_HEADS, GQA_HEAD_DIM),
            (h @ wk).reshape(b, n, GQA_KV_HEADS, GQA_HEAD_DIM),
            (h @ wv).reshape(b, n, GQA_KV_HEADS, GQA_HEAD_DIM))


def cmul(ar, ai, br, bi):
    return ar * br - ai * bi, ar * bi + ai * br


def s5_discretize(lam_re, lam_im, b_re, b_im, log_dt):
    dt = jnp.exp(log_dt.astype(jnp.float32))[..., None]
    lr = lam_re.astype(jnp.float32)
    li = lam_im.astype(jnp.float32)
    mag = jnp.exp(lr * dt)
    ar = mag * jnp.cos(li * dt)
    ai = mag * jnp.sin(li * dt)
    den = lr * lr + li * li
    cr = ((ar - 1.0) * lr + ai * li) / den
    ci = (ai * lr - (ar - 1.0) * li) / den
    bbr, bbi = cmul(cr[..., None], ci[..., None], b_re.astype(jnp.float32), b_im.astype(jnp.float32))
    return ar, ai, bbr, bbi


def s5_scan(u, ar, ai, bbr, bbi, c_re, c_im, h0r, h0i, reverse):
    init_edge = -1 if reverse else 0
    final_edge = 0 if reverse else -1
    xr = jnp.einsum('bngh,gph->bngp', u, bbr)
    xi = jnp.einsum('bngh,gph->bngp', u, bbi)
    ihr, ihi = cmul(ar, ai, h0r, h0i)
    xr = xr.at[:, init_edge].add(ihr)
    xi = xi.at[:, init_edge].add(ihi)
    shape = xr.shape
    elems = (jnp.broadcast_to(ar, shape), jnp.broadcast_to(ai, shape), xr, xi)

    def combine(e1, e2):
        a1r, a1i, b1r, b1i = e1
        a2r, a2i, b2r, b2i = e2
        nar, nai = cmul(a1r, a1i, a2r, a2i)
        tr, ti = cmul(a2r, a2i, b1r, b1i)
        return nar, nai, tr + b2r, ti + b2i

    _, _, hr, hi = lax.associative_scan(combine, elems, reverse=reverse, axis=1)
    y = jnp.einsum('bngp,ghp->bngh', hr, c_re) - jnp.einsum('bngp,ghp->bngh', hi, c_im)
    return y, hr[:, final_edge], hi[:, final_edge]


def s5_chunk(args):
    u, ar, ai, bbr, bbi, c_re, c_im, h0 = args
    yf, fr, fi = s5_scan(u, ar[0], ai[0], bbr[0], bbi[0], c_re[0], c_im[0], h0[:, 0, ..., 0], h0[:, 0, ..., 1], False)
    yb, br, bi = s5_scan(u, ar[1], ai[1], bbr[1], bbi[1], c_re[1], c_im[1], h0[:, 1, ..., 0], h0[:, 1, ..., 1], True)
    h_t = jnp.stack([jnp.stack([fr, fi], axis=-1), jnp.stack([br, bi], axis=-1)], axis=1)
    return yf + yb, h_t


def s5_mixer(h, h0, lam_re, lam_im, b_re, b_im, c_re, c_im, log_dt, d_skip, w_glu, b_glu):
    bsz, n, _ = h.shape
    hf = h.astype(jnp.float32)
    u = hf.reshape(bsz, n, S5_N_CHUNKS, S5_CHUNK, S5_GROUP_CH).transpose(2, 0, 1, 3, 4)
    ar, ai, bbr, bbi = s5_discretize(lam_re, lam_im, b_re, b_im, log_dt)

    def by_chunk(t):
        t = t.reshape((2, S5_N_CHUNKS, S5_CHUNK) + t.shape[2:])
        return jnp.moveaxis(t, 1, 0)

    h0c = jnp.moveaxis(h0.astype(jnp.float32).reshape(bsz, 2, S5_N_CHUNKS, S5_CHUNK, S5_STATE, 2), 2, 0)
    y, h_t = lax.map(s5_chunk, (u, by_chunk(ar), by_chunk(ai), by_chunk(bbr), by_chunk(bbi),
                                 by_chunk(c_re.astype(jnp.float32)), by_chunk(c_im.astype(jnp.float32)), h0c))
    y = jnp.moveaxis(y, 0, 2).reshape(bsz, n, D_MODEL)
    h_t = jnp.moveaxis(h_t, 0, 2).reshape(bsz, 2, S5_GROUPS, S5_STATE, 2)
    y = (y + d_skip.astype(jnp.float32) * hf).astype(h.dtype)
    z = jax.nn.gelu(y) @ w_glu + b_glu
    za, zg = jnp.split(z, 2, axis=-1)
    return za * jax.nn.sigmoid(zg), h_t.astype(h.dtype)


def conv_mixer(h, w_pw1, b_pw1, w_dw, b_dw, ln_g, ln_b, w_pw2, b_pw2):
    z = h @ w_pw1 + b_pw1
    za, zg = jnp.split(z, 2, axis=-1)
    z = za * jax.nn.sigmoid(zg)
    pad = CONV_WIDTH // 2
    z = lax.conv_general_dilated(z, w_dw[:, None, :], window_strides=(1,), padding=[(pad, pad)],
                                 dimension_numbers=('NWC', 'WIO', 'NWC'), feature_group_count=D_MODEL) + b_dw
    z = jax.nn.silu(layernorm(z, ln_g, ln_b))
    return z @ w_pw2 + b_pw2


def moe(h, w_router, b_router, w1, w3, w2):
    bsz, n, d = h.shape
    x = h.reshape(bsz * n, d)
    t = x.shape[0]
    scores = jax.nn.sigmoid((x @ w_router).astype(jnp.float32))
    biased = (scores + b_router.astype(jnp.float32)).reshape(t, N_EXPERT_GROUPS, EXPERTS_PER_GROUP)
    group_score = jnp.sum(lax.top_k(biased, TOP_K)[0], axis=-1)
    g_sel = jnp.argmax(group_score, axis=-1)
    in_group = jnp.take_along_axis(biased, g_sel[:, None, None], axis=1)[:, 0]
    _, local = lax.top_k(in_group, TOP_K)
    expert = g_sel[:, None] * EXPERTS_PER_GROUP + local
    gate = jnp.take_along_axis(scores, expert, axis=1)
    gate = gate / jnp.sum(gate, axis=-1, keepdims=True)
    n_assign = t * TOP_K
    e_flat = expert.reshape(-1)
    order = jnp.argsort(e_flat)
    e_sorted = e_flat[order]
    tok_sorted = order // TOP_K
    gate_sorted = gate.reshape(-1)[order]
    counts = jnp.bincount(e_flat, length=N_EXPERTS)
    starts = jnp.cumsum(counts) - counts
    padded = (counts + MOE_BLOCK - 1) // MOE_BLOCK * MOE_BLOCK
    pends = jnp.cumsum(padded)
    pstarts = pends - padded
    dest = pstarts[e_sorted] + jnp.arange(n_assign) - starts[e_sorted]
    n_blocks = (n_assign + N_EXPERTS * (MOE_BLOCK - 1) + MOE_BLOCK - 1) // MOE_BLOCK
    xbuf = jnp.zeros((n_blocks * MOE_BLOCK, d), x.dtype).at[dest].set(x[tok_sorted])
    block_expert = jnp.minimum(jnp.searchsorted(pends, jnp.arange(n_blocks) * MOE_BLOCK, side='right'), N_EXPERTS - 1)

    def expert_block(args):
        xb, e = args
        return (jax.nn.silu(xb @ w1[e]) * (xb @ w3[e])) @ w2[e]

    ybuf = lax.map(expert_block, (xbuf.reshape(n_blocks, MOE_BLOCK, d), block_expert)).reshape(-1, d)
    y = jnp.zeros_like(x).at[tok_sorted].add(gate_sorted[:, None].astype(x.dtype) * ybuf[dest])
    return y.reshape(bsz, n, d)


def setup_inputs(seed: int = 0) -> dict:
    key = jax.random.key(seed)
    ks = iter(jax.random.split(key, 80))

    def nrm(shape, scale=1.0):
        return jax.random.normal(next(ks), shape, jnp.float32) * scale

    def gain(shape):
        return 1.0 + nrm(shape, 0.02)

    D, R, G, P = D_MODEL, N_REPEAT, S5_GROUPS, S5_STATE
    n_idx = jnp.arange(P, dtype=jnp.float32)
    return {
        'x_prompt': nrm((BATCH, SEQ, D)),
        'x_sample': nrm((DEC_BATCH, DEC_SEQ, D)),
        'cache_mla_ckv': nrm((DEC_BATCH, R, PAST_LEN, MLA_KV_RANK)),
        'cache_mla_krope': nrm((DEC_BATCH, R, PAST_LEN, MLA_ROPE)),
        'cache_gqa_k': nrm((DEC_BATCH, R, PAST_LEN, GQA_KV_HEADS, GQA_HEAD_DIM)),
        'cache_gqa_v': nrm((DEC_BATCH, R, PAST_LEN, GQA_KV_HEADS, GQA_HEAD_DIM)),
        'state_s5': nrm((DEC_BATCH, R, 2, G, P, 2), 0.1),
        'c': nrm((DEC_BATCH, D)),
        'c_ctx': nrm((D,)),
        'norm_mix_g': gain((DEPTH, D)),
        'norm_ffn_g': gain((DEPTH, D)),
        'ada_w': nrm((DEPTH, D, 6 * D), 0.5 * D ** -0.5),
        'ada_b': nrm((DEPTH, 6 * D), 0.02),
        'final_norm_g': gain((D,)),
        'mla_wq_a': nrm((R, D, MLA_Q_RANK), D ** -0.5),
        'mla_q_norm': gain((R, MLA_Q_RANK)),
        'mla_wq_b': nrm((R, MLA_Q_RANK, MLA_HEADS * (MLA_NOPE + MLA_ROPE)), MLA_Q_RANK ** -0.5),
        'mla_wkv_a': nrm((R, D, MLA_KV_RANK + MLA_ROPE), D ** -0.5),
        'mla_kv_norm': gain((R, MLA_KV_RANK)),
        'mla_wkv_b': nrm((R, MLA_KV_RANK, MLA_HEADS * (MLA_NOPE + MLA_V)), MLA_KV_RANK ** -0.5),
        'mla_wo': nrm((R, MLA_HEADS * MLA_V, D), (MLA_HEADS * MLA_V) ** -0.5),
        'gqa_wq': nrm((R, D, GQA_HEADS * GQA_HEAD_DIM), D ** -0.5),
        'gqa_wk': nrm((R, D, GQA_KV_HEADS * GQA_HEAD_DIM), D ** -0.5),
        'gqa_wv': nrm((R, D, GQA_KV_HEADS * GQA_HEAD_DIM), D ** -0.5),
        'gqa_wo': nrm((R, GQA_HEADS * GQA_HEAD_DIM, D), (GQA_HEADS * GQA_HEAD_DIM) ** -0.5),
        'gqa_sink': nrm((R, GQA_HEADS), 0.5),
        's5_lam_re': -0.5 * (1.0 + nrm((R, 2, G, P), 0.05)),
        's5_lam_im': jnp.pi * n_idx + nrm((R, 2, G, P), 0.05),
        's5_b_re': nrm((R, 2, G, P, S5_GROUP_CH), (2.0 * S5_GROUP_CH) ** -0.5),
        's5_b_im': nrm((R, 2, G, P, S5_GROUP_CH), (2.0 * S5_GROUP_CH) ** -0.5),
        's5_c_re': nrm((R, 2, G, S5_GROUP_CH, P), P ** -0.5),
        's5_c_im': nrm((R, 2, G, S5_GROUP_CH, P), P ** -0.5),
        's5_log_dt': jax.random.uniform(next(ks), (R, 2, G), jnp.float32, math.log(DT_MIN), math.log(DT_MAX)),
        's5_d': nrm((R, D)),
        's5_w_glu': nrm((R, D, 2 * D), D ** -0.5),
        's5_b_glu': nrm((R, 2 * D), 0.02),
        'conv_w_pw1': nrm((R, D, 2 * D), D ** -0.5),
        'conv_b_pw1': nrm((R, 2 * D), 0.02),
        'conv_w_dw': nrm((R, CONV_WIDTH, D), CONV_WIDTH ** -0.5),
        'conv_b_dw': nrm((R, D), 0.02),
        'conv_ln_g': gain((R, D)),
        'conv_ln_b': nrm((R, D), 0.02),
        'conv_w_pw2': nrm((R, D, D), D ** -0.5),
        'conv_b_pw2': nrm((R, D), 0.02),
        'moe_w_router': nrm((D, N_EXPERTS), D ** -0.5),
        'moe_b_router': nrm((N_EXPERTS,), 0.01),
        'moe_w1': nrm((DEPTH, N_EXPERTS, D, D_EXPERT), D ** -0.5),
        'moe_w3': nrm((DEPTH, N_EXPERTS, D, D_EXPERT), D ** -0.5),
        'moe_w2': nrm((DEPTH, N_EXPERTS, D_EXPERT, D), D_EXPERT ** -0.5),
    }


def reference(x_prompt, x_sample, cache_mla_ckv, cache_mla_krope, cache_gqa_k, cache_gqa_v, state_s5, c, c_ctx,
              norm_mix_g, norm_ffn_g, ada_w, ada_b, final_norm_g,
              mla_wq_a, mla_q_norm, mla_wq_b, mla_wkv_a, mla_kv_norm, mla_wkv_b, mla_wo,
              gqa_wq, gqa_wk, gqa_wv, gqa_wo, gqa_sink,
              s5_lam_re, s5_lam_im, s5_b_re, s5_b_im, s5_c_re, s5_c_im, s5_log_dt, s5_d, s5_w_glu, s5_b_glu,
              conv_w_pw1, conv_b_pw1, conv_w_dw, conv_b_dw, conv_ln_g, conv_ln_b, conv_w_pw2, conv_b_pw2,
              moe_w_router, moe_b_router, moe_w1, moe_w3, moe_w2):
    xp, xs = x_prompt, x_sample
    n_lat = xs.shape[1]
    mla_cos, mla_sin = axial_rope(n_lat, MLA_ROPE, xs.dtype)
    gqa_cos, gqa_sin = axial_rope(n_lat, GQA_HEAD_DIM, xs.dtype)
    ckv_out, krope_out, k_out, v_out, s5_out = [], [], [], [], []
    for layer in range(DEPTH):
        kind, r = layer % N_MIXERS, layer // N_MIXERS
        mod_p = modulation(c_ctx, ada_w[layer], ada_b[layer])
        mod_s = modulation(c, ada_w[layer], ada_b[layer])
        hp = adaln(xp, norm_mix_g[layer], mod_p[0], mod_p[1])
        hs = adaln(xs, norm_mix_g[layer], mod_s[0], mod_s[1])
        if kind == 0:
            q_p, ckv_p, kr_p = mla_project(hp, mla_wq_a[r], mla_q_norm[r], mla_wq_b[r], mla_wkv_a[r], mla_kv_norm[r])
            k_p, v_p = mla_expand(ckv_p, kr_p, mla_wkv_b[r])
            out_p = dense_attention(q_p, k_p, v_p) @ mla_wo[r]
            q_s, ckv_s, kr_s = mla_project(hs, mla_wq_a[r], mla_q_norm[r], mla_wq_b[r], mla_wkv_a[r], mla_kv_norm[r])
            q_s = jnp.concatenate([q_s[..., :MLA_NOPE], apply_rope(q_s[..., MLA_NOPE:], mla_cos, mla_sin)], axis=-1)
            kr_s = apply_rope(kr_s[:, :, None, :], mla_cos, mla_sin)[:, :, 0, :]
            k_s, v_s = mla_expand(ckv_s, kr_s, mla_wkv_b[r])
            k_c, v_c = mla_expand(cache_mla_ckv[:, r], cache_mla_krope[:, r], mla_wkv_b[r])
            out_s = dense_attention(q_s, jnp.concatenate([k_s, k_c], axis=1),
                                    jnp.concatenate([v_s, v_c], axis=1)) @ mla_wo[r]
            ckv_out.append(ckv_p)
            krope_out.append(kr_p)
        elif kind == 1:
            q_p, k_p, v_p = gqa_project(hp, gqa_wq[r], gqa_wk[r], gqa_wv[r])
            out_p = dense_attention(q_p, k_p, v_p, gqa_sink[r]) @ gqa_wo[r]
            q_s, k_s, v_s = gqa_project(hs, gqa_wq[r], gqa_wk[r], gqa_wv[r])
            out_s = window_attention(apply_rope(q_s, gqa_cos, gqa_sin), apply_rope(k_s, gqa_cos, gqa_sin), v_s,
                                     cache_gqa_k[:, r], cache_gqa_v[:, r], gqa_sink[r]) @ gqa_wo[r]
            k_out.append(k_p)
            v_out.append(v_p)
        elif kind == 2:
            s5_args = (s5_lam_re[r], s5_lam_im[r], s5_b_re[r], s5_b_im[r], s5_c_re[r], s5_c_im[r],
                       s5_log_dt[r], s5_d[r], s5_w_glu[r], s5_b_glu[r])
            h0_ctx = jnp.zeros((xp.shape[0], 2, S5_GROUPS, S5_STATE, 2), xp.dtype)
            out_p, st_p = s5_mixer(hp, h0_ctx, *s5_args)
            out_s, _ = s5_mixer(hs, state_s5[:, r], *s5_args)
            s5_out.append(st_p)
        else:
            conv_args = (conv_w_pw1[r], conv_b_pw1[r], conv_w_dw[r], conv_b_dw[r], conv_ln_g[r], conv_ln_b[r],
                         conv_w_pw2[r], conv_b_pw2[r])
            out_p = conv_mixer(hp, *conv_args)
            out_s = conv_mixer(hs, *conv_args)
        xp = xp + mod_p[2] * out_p
        xs = xs + mod_s[2] * out_s
        xp = xp + mod_p[5] * moe(adaln(xp, norm_ffn_g[layer], mod_p[3], mod_p[4]),
                                 moe_w_router, moe_b_router, moe_w1[layer], moe_w3[layer], moe_w2[layer])
        xs = xs + mod_s[5] * moe(adaln(xs, norm_ffn_g[layer], mod_s[3], mod_s[4]),
                                 moe_w_router, moe_b_router, moe_w1[layer], moe_w3[layer], moe_w2[layer])
    y_prompt = rmsnorm(xp, final_norm_g)
    y_sample = rmsnorm(xs, final_norm_g)
    new_mla_ckv = jnp.stack(ckv_out, axis=1)
    new_mla_krope = jnp.stack(krope_out, axis=1)
    new_gqa_k = jnp.stack(k_out, axis=1)
    new_gqa_v = jnp.stack(v_out, axis=1)
    new_s5_state = jnp.stack(s5_out, axis=1)
    return (y_prompt, y_sample, new_mla_ckv, new_mla_krope, new_gqa_k, new_gqa_v, new_s5_state)
```

```python
import functools
import math

import jax
import jax.numpy as jnp
from jax import lax
from jax.experimental import pallas as pl
from jax.experimental.pallas import tpu as pltpu

GRID_W = 64
NORM_EPS = 1e-6
ROPE_BASE = 10000.0
NEG_INF = -1e30

MLA_HEADS = 16
MLA_NOPE = 128
MLA_ROPE = 64
MLA_V = 128

GQA_HEADS = 32
GQA_KV_HEADS = 8
GQA_HEAD_DIM = 64
WINDOW = 128
BAND_BLOCK = 128

S5_GROUP_CH = 16
S5_STATE = 64
S5_SCAN_CHUNK = 16

CONV_WIDTH = 31
CONV_HALO = 16

N_EXPERT_GROUPS = 4
TOP_K = 2

LANES = 128
VMEM_LIMIT_BYTES = 56 * 1024 * 1024

BF16 = jnp.bfloat16
F32 = jnp.float32


def _cparams(sem):
    return pltpu.CompilerParams(dimension_semantics=sem, vmem_limit_bytes=VMEM_LIMIT_BYTES)


def _silu(x):
    return x * jax.nn.sigmoid(x)


def _gelu_tanh(x):
    return 0.5 * x * (1.0 + jnp.tanh(math.sqrt(2.0 / math.pi) * (x + 0.044715 * (x * x * x))))


def _rms(x, g):
    return (x * lax.rsqrt(jnp.mean(x * x, axis=-1, keepdims=True) + NORM_EPS)) * g


def _rope_rot(x, cos_t, sin_t):
    w = x.shape[-1]
    reps = w // LANES
    c = jnp.concatenate([cos_t] * reps, axis=1) if reps > 1 else cos_t
    s = jnp.concatenate([sin_t] * reps, axis=1) if reps > 1 else sin_t
    lane = lax.broadcasted_iota(jnp.int32, x.shape, 1)
    first_half = (lane % 64) < 32
    swapped = jnp.where(first_half, pltpu.roll(x, w - 32, 1), pltpu.roll(x, 32, 1))
    return x * c + swapped * s


def _fused_mm_body(*refs, n_x, n_pc, n_pm, n_w, n_ec, n_ek, n_em, n_er, n_et, n_out, prologue, epilogue):
    pos = 0

    def take(n):
        nonlocal pos
        r = refs[pos:pos + n]
        pos += n
        return r

    x_refs = take(n_x)
    pc_refs = take(n_pc)
    pm_refs = take(n_pm)
    w_refs = take(n_w)
    ec_refs = take(n_ec + n_ek)
    em_refs = take(n_em)
    er_refs = take(n_er)
    et_refs = take(n_et)
    out_refs = take(n_out)
    (h_ref,) = take(1)
    i = pl.program_id(0)
    j = pl.program_id(1)

    @pl.when(j == 0)
    def _():
        h = prologue([r[...] for r in x_refs], [r[...] for r in pc_refs], [r[0] for r in pm_refs])
        h_ref[...] = h.astype(h_ref.dtype)

    h = h_ref[...]
    accs = [jnp.dot(h, w_ref[...].astype(BF16), preferred_element_type=F32) for w_ref in w_refs]
    outs = epilogue(accs, i, j, [r[...] for r in ec_refs], [r[0] for r in em_refs],
                    [r[...] for r in er_refs], [r[...] for r in et_refs])
    for o_ref, o in zip(out_refs, outs):
        o_ref[...] = o.astype(o_ref.dtype)


def fused_mm(*, name, m, k, n, tm, tn, slab, xs, prologue, ws, epilogue, outs,
             pro_consts=(), pro_mods=(), epi_cols=(), epi_consts=(), epi_mods=(), epi_rows=(), epi_tabs=()):
    assert m % tm == 0 and n % tn == 0 and slab % tm == 0
    grid = (m // tm, n // tn)
    tiles_per_slab = slab // tm
    in_specs, args = [], []
    for arr, off in xs:
        in_specs.append(pl.BlockSpec((tm, k), functools.partial(lambda i, j, off: (i + off, 0), off=off)))
        args.append(arr)
    for arr in pro_consts:
        in_specs.append(pl.BlockSpec((1, k), lambda i, j: (0, 0)))
        args.append(arr)
    for arr in pro_mods:
        in_specs.append(pl.BlockSpec((1, 1, k), lambda i, j: (i // tiles_per_slab, 0, 0)))
        args.append(arr)
    for arr, off in ws:
        in_specs.append(pl.BlockSpec((k, tn), functools.partial(lambda i, j, off: (0, j + off), off=off)))
        args.append(arr)
    for arr, off in epi_cols:
        in_specs.append(pl.BlockSpec((1, tn), functools.partial(lambda i, j, off: (0, j + off), off=off)))
        args.append(arr)
    for arr in epi_consts:
        in_specs.append(pl.BlockSpec(arr.shape, lambda i, j: (0, 0)))
        args.append(arr)
    for arr in epi_mods:
        in_specs.append(pl.BlockSpec((1, 1, tn), lambda i, j: (i // tiles_per_slab, 0, j)))
        args.append(arr)
    for arr in epi_rows:
        in_specs.append(pl.BlockSpec((tm, tn), lambda i, j: (i, j)))
        args.append(arr)
    for arr in epi_tabs:
        in_specs.append(pl.BlockSpec((tm, arr.shape[1]), lambda i, j: (i % tiles_per_slab, 0)))
        args.append(arr)
    out_specs, out_shapes = [], []
    for cols, dtype, bcols, cfn in outs:
        out_specs.append(pl.BlockSpec((tm, bcols), functools.partial(lambda i, j, cfn: (i, cfn(j)), cfn=cfn)))
        out_shapes.append(jax.ShapeDtypeStruct((m, cols), dtype))
    body = functools.partial(
        _fused_mm_body, n_x=len(xs), n_pc=len(pro_consts), n_pm=len(pro_mods), n_w=len(ws),
        n_ec=len(epi_cols), n_ek=len(epi_consts), n_em=len(epi_mods), n_er=len(epi_rows), n_et=len(epi_tabs),
        n_out=len(outs), prologue=prologue, epilogue=epilogue)
    res = pl.pallas_call(
        body, grid=grid, in_specs=in_specs, out_specs=out_specs, out_shape=out_shapes,
        scratch_shapes=[pltpu.VMEM((tm, k), BF16)],
        compiler_params=_cparams(("parallel", "arbitrary")), name=name)(*args)
    return res


def _pro_cast(xv, cv, mv):
    return xv[0]


def _pro_adaln(xv, cv, mv):
    return _rms(xv[0], cv[0]) * (1.0 + mv[1]) + mv[0]


def _pro_rms(xv, cv, mv):
    return _rms(xv[0], cv[0])


def _pro_ln_silu(xv, cv, mv):
    x = xv[0]
    mu = jnp.mean(x, axis=-1, keepdims=True)
    xc = x - mu
    var = jnp.mean(xc * xc, axis=-1, keepdims=True)
    return _silu((xc * lax.rsqrt(var + NORM_EPS)) * cv[0] + cv[1])


def _pro_s5_post(xv, cv, mv):
    return _gelu_tanh(xv[0] + cv[0] * xv[1])


def _epi_plain(accs, i, j, ec, em, er, et):
    return (accs[0],)


def _epi_bias_residual(accs, i, j, ec, em, er, et):
    return (er[0] + em[0] * (accs[0] + ec[0]),)


def _epi_residual(accs, i, j, ec, em, er, et):
    return (er[0] + em[0] * accs[0],)


def _epi_glu(accs, i, j, ec, em, er, et):
    return ((accs[0] + ec[0]) * jax.nn.sigmoid(accs[1] + ec[1]),)


def _epi_glu_residual(accs, i, j, ec, em, er, et):
    return (er[0] + em[0] * ((accs[0] + ec[0]) * jax.nn.sigmoid(accs[1] + ec[1])),)


def _modulation_body(c_ref, w_ref, b_ref, o_ref):
    h = _silu(c_ref[...]).astype(BF16)
    o_ref[0] = jnp.dot(h, w_ref[0].astype(BF16), preferred_element_type=F32) + b_ref[0]


def modulation_all(cond, ada_w, ada_b, tn):
    depth, d, n = ada_w.shape
    rows = cond.shape[0]
    return pl.pallas_call(
        _modulation_body, grid=(depth, n // tn),
        in_specs=[pl.BlockSpec((rows, d), lambda l, j: (0, 0)),
                  pl.BlockSpec((1, d, tn), lambda l, j: (l, 0, j)),
                  pl.BlockSpec((1, 1, tn), lambda l, j: (l, 0, j))],
        out_specs=pl.BlockSpec((1, rows, tn), lambda l, j: (l, 0, j)),
        out_shape=jax.ShapeDtypeStruct((depth, rows, n), F32),
        compiler_params=_cparams(("parallel", "parallel")), name="modulation")(
            cond, ada_w, ada_b.reshape(depth, 1, n))


def _norm_body(*refs, modulated, router):
    x_ref, g_ref = refs[0], refs[1]
    pos = 2
    y = _rms(x_ref[...], g_ref[...])
    if modulated:
        shift_ref, scale_ref = refs[pos], refs[pos + 1]
        pos += 2
        y = y * (1.0 + scale_ref[0]) + shift_ref[0]
    if router:
        whi_ref, wlo_ref = refs[pos], refs[pos + 1]
        pos += 2
    o_ref = refs[pos]
    pos += 1
    o_ref[...] = y.astype(o_ref.dtype)
    if router:
        l_ref = refs[pos]
        y_hi = y.astype(BF16)
        y_lo = (y - y_hi.astype(F32)).astype(BF16)
        w_hi = whi_ref[...]
        l_ref[...] = (jnp.dot(y_hi, w_hi, preferred_element_type=F32)
                      + jnp.dot(y_lo, w_hi, preferred_element_type=F32)
                      + jnp.dot(y_hi, wlo_ref[...], preferred_element_type=F32))


def norm_rows(x, g, *, tm, slab, out_dtype, shift=None, scale=None, router_w=None, name="norm"):
    m, d = x.shape
    tiles_per_slab = slab // tm
    in_specs = [pl.BlockSpec((tm, d), lambda i: (i, 0)), pl.BlockSpec((1, d), lambda i: (0, 0))]
    args = [x, g]
    modulated = shift is not None
    if modulated:
        in_specs += [pl.BlockSpec((1, 1, d), lambda i: (i // tiles_per_slab, 0, 0))] * 2
        args += [shift, scale]
    out_specs = [pl.BlockSpec((tm, d), lambda i: (i, 0))]
    out_shapes = [jax.ShapeDtypeStruct((m, d), out_dtype)]
    if router_w is not None:
        w_hi, w_lo = router_w
        in_specs += [pl.BlockSpec(w_hi.shape, lambda i: (0, 0))] * 2
        args += [w_hi, w_lo]
        out_specs.append(pl.BlockSpec((tm, LANES), lambda i: (i, 0)))
        out_shapes.append(jax.ShapeDtypeStruct((m, LANES), F32))
    return pl.pallas_call(
        functools.partial(_norm_body, modulated=modulated, router=router_w is not None),
        grid=(m // tm,), in_specs=in_specs, out_specs=out_specs, out_shape=out_shapes,
        compiler_params=_cparams(("parallel",)), name=name)(*args)


def _mla_attn_body(*refs, scale, nk, aliased):
    if aliased:
        refs = refs[1:]
    qn_ref, qr_ref, kn_ref, v_ref, kr_ref, o_ref, m_ref, l_ref, acc_ref = refs
    ki = pl.program_id(3)

    @pl.when(ki == 0)
    def _():
        m_ref[...] = jnp.full(m_ref.shape, -jnp.inf, F32)
        l_ref[...] = jnp.zeros(l_ref.shape, F32)
        acc_ref[...] = jnp.zeros(acc_ref.shape, F32)

    qn, qr, kn, v, kr = qn_ref[...], qr_ref[...], kn_ref[...], v_ref[...], kr_ref[...]
    for hh in range(2):
        q = jnp.concatenate([qn[:, hh * MLA_NOPE:(hh + 1) * MLA_NOPE],
                             qr[:, hh * MLA_ROPE:(hh + 1) * MLA_ROPE]], axis=1)
        k = jnp.concatenate([kn[:, hh * MLA_NOPE:(hh + 1) * MLA_NOPE], kr], axis=1)
        s = lax.dot_general(q, k, (((1,), (1,)), ((), ())), preferred_element_type=F32) * scale
        m_prev = m_ref[hh]
        m_new = jnp.maximum(m_prev, jnp.max(s, axis=1, keepdims=True))
        alpha = jnp.exp(m_prev - m_new)
        p = jnp.exp(s - m_new)
        l_ref[hh] = alpha * l_ref[hh] + jnp.sum(p, axis=1, keepdims=True)
        acc_ref[hh] = alpha * acc_ref[hh] + jnp.dot(
            p.astype(BF16), v[:, hh * MLA_V:(hh + 1) * MLA_V], preferred_element_type=F32)
        m_ref[hh] = m_new

    @pl.when(ki == nk - 1)
    def _():
        o_ref[...] = jnp.concatenate([acc_ref[hh] / l_ref[hh] for hh in range(2)], axis=1).astype(o_ref.dtype)


def mla_attention(qn, qr, kv, kr, *, n_heads, batch, nq_rows, nk_rows, q_row0, k_row0, tq, tk, prev_out=None):
    t = qn.shape[0]
    hp = n_heads // 2
    nq, nk = nq_rows // tq, nk_rows // tk
    assert q_row0 % tq == 0 and k_row0 % tk == 0 and nq_rows % tq == 0 and nk_rows % tk == 0
    qb0, kb0 = q_row0 // tq, k_row0 // tk

    def qmap(b, h, qi, ki):
        return (qb0 + b * nq + qi, h)

    def kmap(b, h, qi, ki):
        return (kb0 + b * nk + ki, h)

    in_specs = [pl.BlockSpec((tq, 2 * MLA_NOPE), qmap),
                pl.BlockSpec((tq, 2 * MLA_ROPE), qmap),
                pl.BlockSpec((tk, 2 * MLA_NOPE), kmap),
                pl.BlockSpec((tk, 2 * MLA_V), lambda b, h, qi, ki: (kb0 + b * nk + ki, hp + h)),
                pl.BlockSpec((tk, MLA_ROPE), lambda b, h, qi, ki: (kb0 + b * nk + ki, 0))]
    args = [qn, qr, kv, kv, kr]
    aliases = {}
    if prev_out is not None:
        in_specs = [pl.BlockSpec(memory_space=pl.ANY)] + in_specs
        args = [prev_out] + args
        aliases = {0: 0}
    dk = MLA_NOPE + MLA_ROPE
    return pl.pallas_call(
        functools.partial(_mla_attn_body, scale=dk ** -0.5, nk=nk, aliased=prev_out is not None),
        grid=(batch, hp, nq, nk), in_specs=in_specs,
        out_specs=pl.BlockSpec((tq, 2 * MLA_V), qmap),
        out_shape=jax.ShapeDtypeStruct((t, n_heads * MLA_V), BF16),
        scratch_shapes=[pltpu.VMEM((2, tq, 1), F32), pltpu.VMEM((2, tq, 1), F32),
                        pltpu.VMEM((2, tq, MLA_V), F32)],
        input_output_aliases=aliases,
        compiler_params=_cparams(("parallel", "parallel", "parallel", "arbitrary")),
        name="mla_attention")(*args)


def _gqa_attn_body(*refs, scale, n_seg, window, nq, aliased, group):
    if aliased:
        refs = refs[1:]
    sink_ref, q_ref = refs[0], refs[1]
    k_refs = refs[2:2 + n_seg]
    v_refs = refs[2 + n_seg:2 + 2 * n_seg]
    o_ref = refs[2 + 2 * n_seg]
    p_idx = pl.program_id(1)
    qi = pl.program_id(2)
    d = GQA_HEAD_DIM
    q = q_ref[...]
    tq = q.shape[0]
    ks = [r[...] for r in k_refs]
    vs = [r[...] for r in v_refs]
    if ks[-1].ndim == 3:
        ks[-1] = ks[-1][0]
        vs[-1] = vs[-1][0]
    n_keys = sum(kk.shape[0] for kk in ks)
    if window:
        row = lax.broadcasted_iota(jnp.int32, (group * tq, n_keys), 0) % tq
        col = lax.broadcasted_iota(jnp.int32, (group * tq, n_keys), 1)
        n_loc = 3 * BAND_BLOCK
        in_band = jnp.abs(row + BAND_BLOCK - col) <= WINDOW
        lo = jnp.where(qi > 0, 0, BAND_BLOCK)
        hi = jnp.where(qi < nq - 1, n_loc, 2 * BAND_BLOCK)
        valid = (col >= n_loc) | (in_band & (col >= lo) & (col < hi))
    outs = []
    for kvh in range(2):
        k = jnp.concatenate([kk[:, kvh * d:(kvh + 1) * d] for kk in ks], axis=0)
        v = jnp.concatenate([vv[:, kvh * d:(kvh + 1) * d] for vv in vs], axis=0)
        base = kvh * group * d
        q4 = jnp.concatenate([q[:, base + g * d: base + (g + 1) * d] for g in range(group)], axis=0)
        s = lax.dot_general(q4, k, (((1,), (1,)), ((), ())), preferred_element_type=F32) * scale
        if window:
            s = jnp.where(valid, s, NEG_INF)
        head0 = (p_idx * 2 + kvh) * group
        snk = jnp.concatenate([jnp.full((tq, 1), sink_ref[head0 + g], F32) for g in range(group)], axis=0)
        m = jnp.maximum(jnp.max(s, axis=1, keepdims=True), snk)
        p = jnp.exp(s - m)
        denom = jnp.sum(p, axis=1, keepdims=True) + jnp.exp(snk - m)
        o4 = jnp.dot(p.astype(BF16), v, preferred_element_type=F32) / denom
        outs += [o4[g * tq:(g + 1) * tq] for g in range(group)]
    o_ref[...] = jnp.concatenate(outs, axis=1).astype(o_ref.dtype)


def gqa_attention(qkv, sink, *, batch, n_rows, q_row0, window, ctx_k=None, ctx_v=None, prev_out=None):
    t = qkv.shape[0]
    d, h, kvh = GQA_HEAD_DIM, GQA_HEADS, GQA_KV_HEADS
    group = h // kvh
    pairs = kvh // 2
    qw = 2 * group * d
    kcol0 = (h * d) // LANES
    vcol0 = (h * d + kvh * d) // LANES
    tq = BAND_BLOCK if window else n_rows
    nq = n_rows // tq
    qb0 = q_row0 // tq
    assert q_row0 % tq == 0

    def qmap(b, p, i):
        return (qb0 + b * nq + i, p)

    in_specs = [pl.BlockSpec(memory_space=pltpu.SMEM), pl.BlockSpec((tq, qw), qmap)]
    args = [sink, qkv]
    if window:
        def kv_specs(col0):
            return [
                pl.BlockSpec((tq, LANES), lambda b, p, i: (qb0 + b * nq + jnp.maximum(i - 1, 0), col0 + p)),
                pl.BlockSpec((tq, LANES), lambda b, p, i: (qb0 + b * nq + i, col0 + p)),
                pl.BlockSpec((tq, LANES), lambda b, p, i: (qb0 + b * nq + jnp.minimum(i + 1, nq - 1), col0 + p)),
            ]
        n_ctx = ctx_k.shape[1]
        ctx_spec = pl.BlockSpec((1, n_ctx, LANES), lambda b, p, i: (b, 0, p))
        in_specs += kv_specs(kcol0) + [ctx_spec] + kv_specs(vcol0) + [ctx_spec]
        args += [qkv, qkv, qkv, ctx_k, qkv, qkv, qkv, ctx_v]
        n_seg = 4
    else:
        in_specs += [pl.BlockSpec((tq, LANES), lambda b, p, i: (qb0 + b * nq + i, kcol0 + p)),
                     pl.BlockSpec((tq, LANES), lambda b, p, i: (qb0 + b * nq + i, vcol0 + p))]
        args += [qkv, qkv]
        n_seg = 1
    aliases = {}
    if prev_out is not None:
        in_specs = [pl.BlockSpec(memory_space=pl.ANY)] + in_specs
        args = [prev_out] + args
        aliases = {0: 0}
    return pl.pallas_call(
        functools.partial(_gqa_attn_body, scale=d ** -0.5, n_seg=n_seg, window=window, nq=nq,
                          aliased=prev_out is not None, group=group),
        grid=(batch, pairs, nq), in_specs=in_specs,
        out_specs=pl.BlockSpec((tq, qw), qmap),
        out_shape=jax.ShapeDtypeStruct((t, h * d), BF16),
        input_output_aliases=aliases,
        compiler_params=_cparams(("parallel", "parallel", "parallel")),
        name="gqa_window_attention" if window else "gqa_dense_attention")(*args)


def _s5_body(u_ref, w1_ref, w2_ref, coef_ref, h0_ref, y_ref, fin_ref, z_ref, sp_ref, *, gb, nb, n_steps):
    lc = S5_SCAN_CHUNK * S5_GROUP_CH
    ns = 2 * S5_STATE
    for g in range(gb):
        z_ref[g] = jnp.dot(u_ref[g], w1_ref[g], preferred_element_type=F32)

    def cmul_add(coef, g, dirn, s, z):
        a1 = coef[g, 2 * dirn:2 * dirn + 1, :]
        a2 = coef[g, 2 * dirn + 1:2 * dirn + 2, :]
        return a1 * s + a2 * pltpu.roll(s, S5_STATE, 1) + z

    coef = coef_ref[...]

    def step(c, carry):
        new = []
        rf = pl.multiple_of(c * nb, nb)
        rb = pl.multiple_of((n_steps - 1 - c) * nb, nb)
        for g in range(gb):
            sf, sb = carry[2 * g], carry[2 * g + 1]
            sp_ref[g, pl.ds(rf, nb), 0:ns] = sf
            sp_ref[g, pl.ds(rb, nb), ns:2 * ns] = sb
            new.append(cmul_add(coef, g, 0, sf, z_ref[g, pl.ds(rf, nb), 0:ns]))
            new.append(cmul_add(coef, g, 1, sb, z_ref[g, pl.ds(rb, nb), ns:2 * ns]))
        return tuple(new)

    init = []
    for g in range(gb):
        init += [h0_ref[g, 0], h0_ref[g, 1]]
    fin = lax.fori_loop(0, n_steps, step, tuple(init))
    for g in range(gb):
        fin_ref[g, 0] = fin[2 * g]
        fin_ref[g, 1] = fin[2 * g + 1]
        y_ref[g] = (jnp.dot(u_ref[g], w2_ref[g, 0:lc, :], preferred_element_type=F32)
                    + jnp.dot(sp_ref[g].astype(BF16), w2_ref[g, lc:, :], preferred_element_type=F32))


def s5_scan(u, w1, w2, coef, h0, *, gb, nb):
    g, cols, lc = u.shape
    n_steps = cols // nb
    ns2 = 4 * S5_STATE
    return pl.pallas_call(
        functools.partial(_s5_body, gb=gb, nb=nb, n_steps=n_steps),
        grid=(g // gb,),
        in_specs=[pl.BlockSpec((gb, cols, lc), lambda i: (i, 0, 0)),
                  pl.BlockSpec((gb, lc, ns2), lambda i: (i, 0, 0)),
                  pl.BlockSpec((gb, lc + ns2, lc), lambda i: (i, 0, 0)),
                  pl.BlockSpec((gb, 4, 2 * S5_STATE), lambda i: (i, 0, 0)),
                  pl.BlockSpec((gb, 2, nb, 2 * S5_STATE), lambda i: (i, 0, 0, 0))],
        out_specs=[pl.BlockSpec((gb, cols, lc), lambda i: (i, 0, 0)),
                   pl.BlockSpec((gb, 2, nb, 2 * S5_STATE), lambda i: (i, 0, 0, 0))],
        out_shape=[jax.ShapeDtypeStruct((g, cols, lc), F32),
                   jax.ShapeDtypeStruct((g, 2, nb, 2 * S5_STATE), F32)],
        scratch_shapes=[pltpu.VMEM((gb, cols, ns2), F32), pltpu.VMEM((gb, cols, ns2), F32)],
        compiler_params=_cparams(("parallel",)), name="s5_scan")(u, w1, w2, coef, h0)


def s5_weights(lam_re, lam_im, b_re, b_im, c_re, c_im, log_dt):
    hp = lax.Precision.HIGHEST
    L, H, P = S5_SCAN_CHUNK, S5_GROUP_CH, S5_STATE
    G = lam_re.shape[1]
    dt = jnp.exp(log_dt)[..., None]
    lr, li = lam_re, lam_im
    mag = jnp.exp(lr * dt)
    ar, ai = mag * jnp.cos(li * dt), mag * jnp.sin(li * dt)
    den = lr * lr + li * li
    cr = ((ar - 1.0) * lr + ai * li) / den
    ci = (ai * lr - (ar - 1.0) * li) / den
    bbr = cr[..., None] * b_re - ci[..., None] * b_im
    bbi = cr[..., None] * b_im + ci[..., None] * b_re
    kk = jnp.arange(L + 1, dtype=F32)[:, None, None, None]
    pmag = jnp.exp(kk * (lr * dt)[None])
    pr, pi = pmag * jnp.cos(kk * (li * dt)[None]), pmag * jnp.sin(kk * (li * dt)[None])
    car = c_re[None] * pr[:, :, :, None, :] - c_im[None] * pi[:, :, :, None, :]
    cai = c_re[None] * pi[:, :, :, None, :] + c_im[None] * pr[:, :, :, None, :]
    mker = (jnp.einsum('ldgop,dgpi->ldgoi', car[:L], bbr, precision=hp)
            - jnp.einsum('ldgop,dgpi->ldgoi', cai[:L], bbi, precision=hp))
    s_idx = jnp.arange(L)[:, None]
    j_idx = jnp.arange(L)[None, :]

    def toeplitz(m, lag, mask):
        t = m[jnp.clip(lag, 0, L - 1)] * mask[:, :, None, None, None]
        return t.transpose(2, 0, 4, 1, 3)

    t_all = (toeplitz(mker[:, 0], j_idx - s_idx, (j_idx >= s_idx).astype(F32))
             + toeplitz(mker[:, 1], s_idx - j_idx, (s_idx >= j_idx).astype(F32))).reshape(G, L * H, L * H)

    def bc(dirn, powers):
        qr_, qi_ = pr[powers, dirn], pi[powers, dirn]
        re = qr_[..., None] * bbr[dirn][None] - qi_[..., None] * bbi[dirn][None]
        im = qr_[..., None] * bbi[dirn][None] + qi_[..., None] * bbr[dirn][None]
        return jnp.concatenate([re, im], axis=2).transpose(1, 0, 3, 2).reshape(G, L * H, 2 * P)

    def cc(dirn, powers):
        wr = car[powers, dirn]
        wi = cai[powers, dirn]
        return jnp.concatenate([wr, -wi], axis=3).transpose(1, 3, 0, 2).reshape(G, 2 * P, L * H)

    ar_l = jnp.arange(L)
    w1 = jnp.concatenate([bc(0, L - 1 - ar_l), bc(1, ar_l)], axis=2)
    w2 = jnp.concatenate([t_all, cc(0, ar_l + 1), cc(1, L - ar_l)], axis=1)
    coef = jnp.stack([jnp.concatenate([pr[L, 0], pr[L, 0]], -1), jnp.concatenate([-pi[L, 0], pi[L, 0]], -1),
                      jnp.concatenate([pr[L, 1], pr[L, 1]], -1), jnp.concatenate([-pi[L, 1], pi[L, 1]], -1)], axis=1)
    return w1.astype(BF16), w2.astype(BF16), coef


def _dwconv_body(cur_ref, prev_ref, next_ref, w_ref, b_ref, o_ref, zp_ref, *, tile, slab, seq_prompt, rows_sub):
    t = pl.program_id(0)
    row0 = t * tile
    seq_len = jnp.where(row0 < slab, seq_prompt, slab)
    pos = row0 % seq_len
    has_prev = pos != 0
    has_next = pos + tile != seq_len
    halo = CONV_HALO
    zp_ref[0:halo, :] = jnp.where(has_prev, prev_ref[...], 0.0)
    zp_ref[halo:halo + tile, :] = cur_ref[...]
    zp_ref[halo + tile:, :] = jnp.where(has_next, next_ref[...], 0.0)
    w = w_ref[...]
    off = halo - CONV_WIDTH // 2
    for r0 in range(0, tile, rows_sub):
        acc = jnp.zeros((rows_sub, w.shape[1]), F32) + b_ref[...]
        for kk in range(CONV_WIDTH):
            acc = acc + w[kk:kk + 1, :] * zp_ref[r0 + off + kk:r0 + off + kk + rows_sub, :]
        o_ref[r0:r0 + rows_sub, :] = acc


def dwconv(z, w, b, *, tile, cb, slab, seq_prompt, rows_sub=32):
    m, d = z.shape
    halo = CONV_HALO
    hb = tile // halo
    n_halo_blocks = m // halo
    assert seq_prompt % tile == 0 and slab % tile == 0 and tile % halo == 0
    return pl.pallas_call(
        functools.partial(_dwconv_body, tile=tile, slab=slab, seq_prompt=seq_prompt, rows_sub=rows_sub),
        grid=(m // tile, d // cb),
        in_specs=[pl.BlockSpec((tile, cb), lambda t, c: (t, c)),
                  pl.BlockSpec((halo, cb), lambda t, c: (jnp.maximum(t * hb - 1, 0), c)),
                  pl.BlockSpec((halo, cb), lambda t, c: (jnp.minimum((t + 1) * hb, n_halo_blocks - 1), c)),
                  pl.BlockSpec((CONV_WIDTH, cb), lambda t, c: (0, c)),
                  pl.BlockSpec((1, cb), lambda t, c: (0, c))],
        out_specs=pl.BlockSpec((tile, cb), lambda t, c: (t, c)),
        out_shape=jax.ShapeDtypeStruct((m, d), F32),
        scratch_shapes=[pltpu.VMEM((tile + 2 * halo, cb), F32)],
        compiler_params=_cparams(("parallel", "parallel")), name="dwconv")(z, z, z, w, b)


def _moe_body(be_ref, nu_ref, x_ref, w1_ref, w3_ref, w2_ref, o_ref, w1s, w3s, w2s):
    b = pl.program_id(0)
    prev = be_ref[jnp.maximum(b - 1, 0)]

    @pl.when((b == 0) | (be_ref[b] != prev))
    def _():
        w1s[...] = w1_ref[0].astype(BF16)
        w3s[...] = w3_ref[0].astype(BF16)
        w2s[...] = w2_ref[0].astype(BF16)

    @pl.when(b < nu_ref[0])
    def _():
        x = x_ref[...]
        a = jnp.dot(x, w1s[...], preferred_element_type=F32)
        g = jnp.dot(x, w3s[...], preferred_element_type=F32)
        o_ref[...] = jnp.dot((_silu(a) * g).astype(BF16), w2s[...], preferred_element_type=F32)

    @pl.when(b >= nu_ref[0])
    def _():
        o_ref[...] = jnp.zeros(o_ref.shape, o_ref.dtype)


def moe_experts(xg, block_expert, n_used, w1, w3, w2, *, tm):
    mp, d = xg.shape
    e, _, f = w1.shape
    grid_spec = pltpu.PrefetchScalarGridSpec(
        num_scalar_prefetch=2, grid=(mp // tm,),
        in_specs=[pl.BlockSpec((tm, d), lambda b, be, nu: (b, 0)),
                  pl.BlockSpec((1, d, f), lambda b, be, nu: (be[b], 0, 0)),
                  pl.BlockSpec((1, d, f), lambda b, be, nu: (be[b], 0, 0)),
                  pl.BlockSpec((1, f, d), lambda b, be, nu: (be[b], 0, 0))],
        out_specs=pl.BlockSpec((tm, d), lambda b, be, nu: (b, 0)),
        scratch_shapes=[pltpu.VMEM((d, f), BF16), pltpu.VMEM((d, f), BF16), pltpu.VMEM((f, d), BF16)])
    return pl.pallas_call(
        _moe_body, grid_spec=grid_spec, out_shape=jax.ShapeDtypeStruct((mp, d), F32),
        compiler_params=_cparams(("arbitrary",)), name="moe_experts")(block_expert, n_used, xg, w1, w3, w2)


def _combine_body(x_ref, y1_ref, y2_ref, g_ref, gm_ref, o_ref):
    g = g_ref[...]
    o_ref[...] = x_ref[...] + gm_ref[0] * (g[:, 0:1] * y1_ref[...] + g[:, 1:2] * y2_ref[...])


def moe_combine(x, y1, y2, gates, gate_mod, *, tm, slab):
    m, d = x.shape
    tiles_per_slab = slab // tm
    row = pl.BlockSpec((tm, d), lambda i: (i, 0))
    return pl.pallas_call(
        _combine_body, grid=(m // tm,),
        in_specs=[row, row, row, pl.BlockSpec((tm, LANES), lambda i: (i, 0)),
                  pl.BlockSpec((1, 1, d), lambda i: (i // tiles_per_slab, 0, 0))],
        out_specs=row, out_shape=jax.ShapeDtypeStruct((m, d), F32),
        compiler_params=_cparams(("parallel",)), name="moe_combine")(x, y1, y2, gates, gate_mod)


def moe_layer(x, norm_g, shift, scale, gate_mod, router_w, b_router, w1, w3, w2, *, tm_rows, tm_moe, slab):
    t, d = x.shape
    n_exp = w1.shape[0]
    epg = n_exp // N_EXPERT_GROUPS
    hb, logits = norm_rows(x, norm_g, tm=tm_rows, slab=slab, out_dtype=BF16, shift=shift, scale=scale,
                           router_w=router_w, name="adaln_router")
    scores = jax.nn.sigmoid(logits[:, :n_exp])
    biased = (scores + b_router).reshape(t, N_EXPERT_GROUPS, epg)
    group_score = jnp.sum(lax.top_k(biased, TOP_K)[0], axis=-1)
    g_sel = jnp.argmax(group_score, axis=-1)
    in_group = jnp.take_along_axis(biased, g_sel[:, None, None], axis=1)[:, 0]
    _, local = lax.top_k(in_group, TOP_K)
    expert = (g_sel[:, None] * epg + local).astype(jnp.int32)
    gate = jnp.take_along_axis(scores, expert, axis=1)
    gate = gate / jnp.sum(gate, axis=-1, keepdims=True)
    e_flat = expert.reshape(-1)
    onehot = (e_flat[:, None] == jnp.arange(n_exp, dtype=jnp.int32)[None, :]).astype(jnp.int32)
    csum = jnp.cumsum(onehot, axis=0)
    rank = jnp.take_along_axis(csum, e_flat[:, None], axis=1)[:, 0] - 1
    counts = csum[-1]
    padded = (counts + tm_moe - 1) // tm_moe * tm_moe
    pends = jnp.cumsum(padded)
    pstarts = pends - padded
    dest = (pstarts[e_flat] + rank).astype(jnp.int32)
    n_assign = t * TOP_K
    n_blocks = (n_assign + n_exp * (tm_moe - 1) + tm_moe - 1) // tm_moe
    block_expert = jnp.minimum(
        jnp.searchsorted(pends, jnp.arange(n_blocks, dtype=jnp.int32) * tm_moe, side='right'),
        n_exp - 1).astype(jnp.int32)
    n_used = (pends[-1] // tm_moe).astype(jnp.int32).reshape(1)
    tok = jnp.arange(n_assign, dtype=jnp.int32) // TOP_K
    src = jnp.zeros((n_blocks * tm_moe,), jnp.int32).at[dest].set(tok)
    xg = jnp.take(hb, src, axis=0)
    yg = moe_experts(xg, block_expert, n_used, w1, w3, w2, tm=tm_moe)
    dest2 = dest.reshape(t, TOP_K)
    y1 = jnp.take(yg, dest2[:, 0], axis=0)
    y2 = jnp.take(yg, dest2[:, 1], axis=0)
    gates = jnp.pad(gate, ((0, 0), (0, LANES - TOP_K)))
    return moe_combine(x, y1, y2, gates, gate_mod, tm=tm_rows, slab=slab)


def _rope_tables(n_tokens):
    rows = n_tokens // GRID_W
    r = jnp.repeat(jnp.arange(rows), GRID_W).astype(F32)
    col = jnp.tile(jnp.arange(GRID_W), rows).astype(F32)
    n_freq = 64 // 4
    inv = ROPE_BASE ** (-jnp.arange(n_freq, dtype=F32) / n_freq)
    ang = jnp.concatenate([r[:, None] * inv, col[:, None] * inv], axis=-1)
    c, s = jnp.cos(ang), jnp.sin(ang)
    return jnp.concatenate([c, c, c, c], axis=1), jnp.concatenate([-s, s, -s, s], axis=1)


def _pick_tile(n, pref):
    t = pref
    while n % t:
        t //= 2
    return t


def kernel(x_prompt, x_sample, cache_mla_ckv, cache_mla_krope, cache_gqa_k, cache_gqa_v, state_s5, c, c_ctx, norm_mix_g, norm_ffn_g, ada_w, ada_b, final_norm_g, mla_wq_a, mla_q_norm, mla_wq_b, mla_wkv_a, mla_kv_norm, mla_wkv_b, mla_wo, gqa_wq, gqa_wk, gqa_wv, gqa_wo, gqa_sink, s5_lam_re, s5_lam_im, s5_b_re, s5_b_im, s5_c_re, s5_c_im, s5_log_dt, s5_d, s5_w_glu, s5_b_glu, conv_w_pw1, conv_b_pw1, conv_w_dw, conv_b_dw, conv_ln_g, conv_ln_b, conv_w_pw2, conv_b_pw2, moe_w_router, moe_b_router, moe_w1, moe_w3, moe_w2):
    bp, sp, d = x_prompt.shape
    bs, ns, _ = x_sample.shape
    depth = ada_w.shape[0]
    past = cache_mla_ckv.shape[2]
    slab = ns
    assert bp * sp == slab, "prompt rows must fill exactly one slab"
    n_slab = 1 + bs
    t = n_slab * slab
    tm = _pick_tile(slab, 1024)
    tmh = _pick_tile(slab, 512)
    tn = _pick_tile(d, 512)
    x = jnp.concatenate([x_prompt.reshape(slab, d), x_sample.reshape(bs * slab, d)], axis=0)

    rows_pad = -(-n_slab // 16) * 16
    cond = jnp.concatenate([c_ctx[None], c, jnp.zeros((rows_pad - n_slab, d), F32)], axis=0)
    mod = modulation_all(cond, ada_w, ada_b, _pick_tile(6 * d, 1024))
    mod = mod[:, :n_slab].reshape(depth, n_slab, 6, 1, d).transpose(0, 2, 1, 3, 4)

    cos_t, sin_t = _rope_tables(ns)
    w_r = jnp.pad(moe_w_router, ((0, 0), (0, LANES - moe_w_router.shape[1])))
    w_r_hi = w_r.astype(BF16)
    w_r_lo = (w_r - w_r_hi.astype(F32)).astype(BF16)
    n_mixers = 4
    outs = {}

    for layer in range(depth):
        kind, r = layer % n_mixers, layer // n_mixers
        sh_mix, sc_mix, gt_mix, sh_ffn, sc_ffn, gt_ffn = [mod[layer, q] for q in range(6)]
        g_mix = norm_mix_g[layer][None]
        if kind == 0:
            qrank = mla_wq_a.shape[2]
            kvrank = mla_kv_norm.shape[1]
            hds = MLA_HEADS
            wa = jnp.concatenate([mla_wq_a[r], mla_wkv_a[r],
                                  jnp.zeros((d, LANES - MLA_ROPE), F32)], axis=1).astype(BF16)
            na = wa.shape[1]
            q_norm, kv_norm = mla_q_norm[r][None], mla_kv_norm[r][None]

            def epi_a(accs, i, j, ec, em, er, et, qrank=qrank, kvrank=kvrank, tiles=slab // tmh):
                a = accs[0]
                qn = _rms(a[:, :qrank], ec[0])
                ckv = _rms(a[:, qrank:qrank + kvrank], ec[1])
                kr = a[:, qrank + kvrank:]
                kr = jnp.where(i >= tiles, _rope_rot(kr, et[0], et[1]), kr)
                return qn, ckv, kr

            qn, ckv, kr = fused_mm(
                name="mla_proj_a", m=t, k=d, n=na, tm=tmh, tn=na, slab=slab,
                xs=[(x, 0)], pro_consts=[g_mix], pro_mods=[sh_mix, sc_mix], prologue=_pro_adaln,
                ws=[(wa, 0)], epi_tabs=[cos_t, sin_t], epi_consts=[q_norm, kv_norm], epilogue=epi_a,
                outs=[(qrank, BF16, qrank, lambda j: 0), (kvrank, F32, kvrank, lambda j: 0),
                      (LANES, F32, LANES, lambda j: 0)])
            outs['ckv'] = ckv[:slab].reshape(bp, 1, sp, kvrank)
            outs['krope'] = kr[:slab, :MLA_ROPE].reshape(bp, 1, sp, MLA_ROPE)
            wqb = mla_wq_b[r].reshape(qrank, hds, MLA_NOPE + MLA_ROPE)
            wqb_n = wqb[:, :, :MLA_NOPE].reshape(qrank, hds * MLA_NOPE).astype(BF16)
            wqb_r = wqb[:, :, MLA_NOPE:].reshape(qrank, hds * MLA_ROPE).astype(BF16)
            (q_nope,) = fused_mm(
                name="mla_q_nope", m=t, k=qrank, n=hds * MLA_NOPE, tm=tm, tn=_pick_tile(hds * MLA_NOPE, 1024),
                slab=slab, xs=[(qn, 0)], prologue=_pro_cast, ws=[(wqb_n, 0)], epilogue=_epi_plain,
                outs=[(hds * MLA_NOPE, BF16, _pick_tile(hds * MLA_NOPE, 1024), lambda j: j)])

            def epi_qr(accs, i, j, ec, em, er, et, tiles=slab // tm):
                a = accs[0]
                return (jnp.where(i >= tiles, _rope_rot(a, et[0], et[1]), a),)

            (q_rope,) = fused_mm(
                name="mla_q_rope", m=t, k=qrank, n=hds * MLA_ROPE, tm=tm, tn=hds * MLA_ROPE,
                slab=slab, xs=[(qn, 0)], prologue=_pro_cast, ws=[(wqb_r, 0)], epilogue=epi_qr,
                epi_tabs=[cos_t, sin_t], outs=[(hds * MLA_ROPE, BF16, hds * MLA_ROPE, lambda j: 0)])
            ckv_s = ckv[slab:].reshape(bs, ns, kvrank)
            kr_s = kr[slab:, :MLA_ROPE].reshape(bs, ns, MLA_ROPE)
            ckv_all = jnp.concatenate(
                [ckv[:slab], jnp.concatenate([ckv_s, cache_mla_ckv[:, r]], axis=1).reshape(-1, kvrank)],
                axis=0).astype(BF16)
            kr_all = jnp.concatenate(
                [kr[:slab, :MLA_ROPE],
                 jnp.concatenate([kr_s, cache_mla_krope[:, r]], axis=1).reshape(-1, MLA_ROPE)],
                axis=0).astype(BF16)
            wkvb = mla_wkv_b[r].reshape(kvrank, hds, MLA_NOPE + MLA_V)
            wkvb = jnp.concatenate([wkvb[:, :, :MLA_NOPE].reshape(kvrank, hds * MLA_NOPE),
                                    wkvb[:, :, MLA_NOPE:].reshape(kvrank, hds * MLA_V)], axis=1).astype(BF16)
            tk_rows = ckv_all.shape[0]
            nkv = wkvb.shape[1]
            tm_kv = _pick_tile(math.gcd(slab, ns + past), 1024)
            (kv_all,) = fused_mm(
                name="mla_kv_expand", m=tk_rows, k=kvrank, n=nkv, tm=tm_kv, tn=_pick_tile(nkv, 1024),
                slab=tm_kv, xs=[(ckv_all, 0)], prologue=_pro_cast, ws=[(wkvb, 0)], epilogue=_epi_plain,
                outs=[(nkv, BF16, _pick_tile(nkv, 1024), lambda j: j)])
            att = mla_attention(q_nope, q_rope, kv_all, kr_all, n_heads=hds, batch=bp, nq_rows=sp, nk_rows=sp,
                                q_row0=0, k_row0=0, tq=sp, tk=sp)
            tqs = _pick_tile(ns, 512)
            tks = _pick_tile(math.gcd(ns + past, slab), 512)
            att = mla_attention(q_nope, q_rope, kv_all, kr_all, n_heads=hds, batch=bs, nq_rows=ns,
                                nk_rows=ns + past, q_row0=slab, k_row0=slab, tq=tqs, tk=tks, prev_out=att)
            (x,) = fused_mm(
                name="mla_out_proj", m=t, k=hds * MLA_V, n=d, tm=tm, tn=tn, slab=slab,
                xs=[(att, 0)], prologue=_pro_cast, ws=[(mla_wo[r].astype(BF16), 0)],
                epi_mods=[gt_mix], epi_rows=[x], epilogue=_epi_residual,
                outs=[(d, F32, tn, lambda j: j)])
        elif kind == 1:
            hq, hkv, hd = GQA_HEADS, GQA_KV_HEADS, GQA_HEAD_DIM
            wqkv = jnp.concatenate([gqa_wq[r], gqa_wk[r], gqa_wv[r]], axis=1).astype(BF16)
            nqkv = wqkv.shape[1]
            tn_qkv = hkv * hd
            n_rope_tiles = (hq * hd + hkv * hd) // tn_qkv
            n_q_tiles = (hq * hd) // tn_qkv

            def epi_qkv(accs, i, j, ec, em, er, et, tiles=slab // tmh, n_rope_tiles=n_rope_tiles):
                a = accs[0]
                roped = jnp.where((i >= tiles) & (j < n_rope_tiles), _rope_rot(a, et[0], et[1]), a)
                return roped, a

            qkv, kv_f32 = fused_mm(
                name="gqa_qkv_proj", m=t, k=d, n=nqkv, tm=tmh, tn=tn_qkv, slab=slab,
                xs=[(x, 0)], pro_consts=[g_mix], pro_mods=[sh_mix, sc_mix], prologue=_pro_adaln,
                ws=[(wqkv, 0)], epi_tabs=[cos_t, sin_t], epilogue=epi_qkv,
                outs=[(nqkv, BF16, tn_qkv, lambda j: j),
                      (2 * tn_qkv, F32, tn_qkv, functools.partial(lambda j, nq: jnp.maximum(j - nq, 0), nq=n_q_tiles))])
            outs['gqa_k'] = kv_f32[:slab, :hkv * hd].reshape(bp, 1, sp, hkv, hd)
            outs['gqa_v'] = kv_f32[:slab, hkv * hd:].reshape(bp, 1, sp, hkv, hd)
            att = gqa_attention(qkv, gqa_sink[r], batch=bp, n_rows=sp, q_row0=0, window=False)
            ctx_k = cache_gqa_k[:, r].reshape(bs, past, hkv * hd).astype(BF16)
            ctx_v = cache_gqa_v[:, r].reshape(bs, past, hkv * hd).astype(BF16)
            att = gqa_attention(qkv, gqa_sink[r], batch=bs, n_rows=ns, q_row0=slab, window=True,
                                ctx_k=ctx_k, ctx_v=ctx_v, prev_out=att)
            (x,) = fused_mm(
                name="gqa_out_proj", m=t, k=hq * hd, n=d, tm=tm, tn=tn, slab=slab,
                xs=[(att, 0)], prologue=_pro_cast, ws=[(gqa_wo[r].astype(BF16), 0)],
                epi_mods=[gt_mix], epi_rows=[x], epilogue=_epi_residual,
                outs=[(d, F32, tn, lambda j: j)])
        elif kind == 2:
            grp, hch, pst, lch = d // S5_GROUP_CH, S5_GROUP_CH, S5_STATE, S5_SCAN_CHUNK
            (hf,) = norm_rows(x, g_mix, tm=tmh, slab=slab, out_dtype=F32, shift=sh_mix, scale=sc_mix, name="adaln_s5")
            w1, w2, coef = s5_weights(s5_lam_re[r], s5_lam_im[r], s5_b_re[r], s5_b_im[r],
                                      s5_c_re[r], s5_c_im[r], s5_log_dt[r])

            def to_chunks(hrows, nbatch, nseq):
                u = hrows.astype(BF16).reshape(nbatch, nseq // lch, lch, grp, hch)
                return u.transpose(3, 1, 0, 2, 4).reshape(grp, (nseq // lch) * nbatch, lch * hch)

            def from_chunks(y, nbatch, nseq):
                y = y.reshape(grp, nseq // lch, nbatch, lch, hch)
                return y.transpose(2, 1, 3, 0, 4).reshape(nbatch * nseq, d)

            h0_p = jnp.zeros((grp, 2, bp, 2 * pst), F32)
            h0_s = state_s5[:, r].transpose(2, 1, 0, 4, 3).reshape(grp, 2, bs, 2 * pst)
            gb = 2 if grp % 2 == 0 else 1
            y_p, fin_p = s5_scan(to_chunks(hf[:slab], bp, sp), w1, w2, coef, h0_p, gb=gb, nb=bp)
            y_s, _ = s5_scan(to_chunks(hf[slab:], bs, ns), w1, w2, coef, h0_s, gb=gb, nb=bs)
            outs['s5'] = fin_p.reshape(grp, 2, bp, 2, pst).transpose(2, 1, 0, 4, 3)[:, None]
            y = jnp.concatenate([from_chunks(y_p, bp, sp), from_chunks(y_s, bs, ns)], axis=0)
            wg = s5_w_glu[r].astype(BF16)
            bg = s5_b_glu[r][None]
            (x,) = fused_mm(
                name="s5_glu", m=t, k=d, n=d, tm=tmh, tn=tn, slab=slab,
                xs=[(y, 0), (hf, 0)], pro_consts=[s5_d[r][None]], prologue=_pro_s5_post,
                ws=[(wg, 0), (wg, d // tn)], epi_cols=[(bg, 0), (bg, d // tn)],
                epi_mods=[gt_mix], epi_rows=[x], epilogue=_epi_glu_residual,
                outs=[(d, F32, tn, lambda j: j)])
        else:
            w1c = conv_w_pw1[r].astype(BF16)
            b1c = conv_b_pw1[r][None]
            (z,) = fused_mm(
                name="conv_pw1_glu", m=t, k=d, n=d, tm=tmh, tn=tn, slab=slab,
                xs=[(x, 0)], pro_consts=[g_mix], pro_mods=[sh_mix, sc_mix], prologue=_pro_adaln,
                ws=[(w1c, 0), (w1c, d // tn)], epi_cols=[(b1c, 0), (b1c, d // tn)], epilogue=_epi_glu,
                outs=[(d, F32, tn, lambda j: j)])
            zc = dwconv(z, conv_w_dw[r], conv_b_dw[r][None], tile=sp, cb=tn, slab=slab, seq_prompt=sp)
            (x,) = fused_mm(
                name="conv_pw2", m=t, k=d, n=d, tm=tmh, tn=tn, slab=slab,
                xs=[(zc, 0)], pro_consts=[conv_ln_g[r][None], conv_ln_b[r][None]], prologue=_pro_ln_silu,
                ws=[(conv_w_pw2[r].astype(BF16), 0)], epi_cols=[(conv_b_pw2[r][None], 0)],
                epi_mods=[gt_mix], epi_rows=[x], epilogue=_epi_bias_residual,
                outs=[(d, F32, tn, lambda j: j)])
        x = moe_layer(x, norm_ffn_g[layer][None], sh_ffn, sc_ffn, gt_ffn, (w_r_hi, w_r_lo), moe_b_router,
                      moe_w1[layer], moe_w3[layer], moe_w2[layer], tm_rows=tmh, tm_moe=256, slab=slab)

    (y,) = norm_rows(x, final_norm_g[None], tm=tmh, slab=slab, out_dtype=F32, name="final_norm")
    return (y[:slab].reshape(bp, sp, d), y[slab:].reshape(bs, ns, d),
            outs['ckv'], outs['krope'], outs['gqa_k'], outs['gqa_v'], outs['s5'])
```

```python
import functools
import math

import jax
import jax.numpy as jnp
from jax import lax
from jax.experimental import pallas as pl
from jax.experimental.pallas import tpu as pltpu

GRID_W = 64
NORM_EPS = 1e-6
ROPE_BASE = 10000.0
NEG_INF = -1e30

MLA_HEADS = 16
MLA_NOPE = 128
MLA_ROPE = 64
MLA_V = 128

GQA_HEADS = 32
GQA_KV_HEADS = 8
GQA_HEAD_DIM = 64
WINDOW = 128
BAND_BLOCK = 128

S5_GROUP_CH = 16
S5_STATE = 64
S5_SCAN_CHUNK = 16

CONV_WIDTH = 31
CONV_HALO = 16

N_EXPERT_GROUPS = 4
TOP_K = 2

LANES = 128
VMEM_LIMIT_BYTES = 56 * 1024 * 1024

BF16 = jnp.bfloat16
F32 = jnp.float32


def _cparams(sem):
    return pltpu.CompilerParams(dimension_semantics=sem, vmem_limit_bytes=VMEM_LIMIT_BYTES)


def _silu(x):
    return x * jax.nn.sigmoid(x)


def _gelu_tanh(x):
    return 0.5 * x * (1.0 + jnp.tanh(math.sqrt(2.0 / math.pi) * (x + 0.044715 * (x * x * x))))


def _rms(x, g):
    return (x * lax.rsqrt(jnp.mean(x * x, axis=-1, keepdims=True) + NORM_EPS)) * g


def _rope_rot(x, cos_t, sin_t):
    w = x.shape[-1]
    reps = w // LANES
    c = jnp.concatenate([cos_t] * reps, axis=1) if reps > 1 else cos_t
    s = jnp.concatenate([sin_t] * reps, axis=1) if reps > 1 else sin_t
    lane = lax.broadcasted_iota(jnp.int32, x.shape, 1)
    first_half = (lane % 64) < 32
    swapped = jnp.where(first_half, pltpu.roll(x, w - 32, 1), pltpu.roll(x, 32, 1))
    return x * c + swapped * s


def _fused_mm_body(*refs, n_x, n_pc, n_pm, n_w, n_ec, n_ek, n_em, n_er, n_et, n_out, prologue, epilogue):
    pos = 0

    def take(n):
        nonlocal pos
        r = refs[pos:pos + n]
        pos += n
        return r

    x_refs = take(n_x)
    pc_refs = take(n_pc)
    pm_refs = take(n_pm)
    w_refs = take(n_w)
    ec_refs = take(n_ec + n_ek)
    em_refs = take(n_em)
    er_refs = take(n_er)
    et_refs = take(n_et)
    out_refs = take(n_out)
    (h_ref,) = take(1)
    i = pl.program_id(0)
    j = pl.program_id(1)

    @pl.when(j == 0)
    def _():
        h = prologue([r[...] for r in x_refs], [r[...] for r in pc_refs], [r[0] for r in pm_refs])
        h_ref[...] = h.astype(h_ref.dtype)

    h = h_ref[...]
    accs = [jnp.dot(h, w_ref[...].astype(BF16), preferred_element_type=F32) for w_ref in w_refs]
    outs = epilogue(accs, i, j, [r[...] for r in ec_refs], [r[0] for r in em_refs],
                    [r[...] for r in er_refs], [r[...] for r in et_refs])
    for o_ref, o in zip(out_refs, outs):
        o_ref[...] = o.astype(o_ref.dtype)


def fused_mm(*, name, m, k, n, tm, tn, slab, xs, prologue, ws, epilogue, outs,
             pro_consts=(), pro_mods=(), epi_cols=(), epi_consts=(), epi_mods=(), epi_rows=(), epi_tabs=()):
    assert m % tm == 0 and n % tn == 0 and slab % tm == 0
    grid = (m // tm, n // tn)
    tiles_per_slab = slab // tm
    in_specs, args = [], []
    for arr, off in xs:
        in_specs.append(pl.BlockSpec((tm, k), functools.partial(lambda i, j, off: (i + off, 0), off=off)))
        args.append(arr)
    for arr in pro_consts:
        in_specs.append(pl.BlockSpec((1, k), lambda i, j: (0, 0)))
        args.append(arr)
    for arr in pro_mods:
        in_specs.append(pl.BlockSpec((1, 1, k), lambda i, j: (i // tiles_per_slab, 0, 0)))
        args.append(arr)
    for arr, off in ws:
        in_specs.append(pl.BlockSpec((k, tn), functools.partial(lambda i, j, off: (0, j + off), off=off)))
        args.append(arr)
    for arr, off in epi_cols:
        in_specs.append(pl.BlockSpec((1, tn), functools.partial(lambda i, j, off: (0, j + off), off=off)))
        args.append(arr)
    for arr in epi_consts:
        in_specs.append(pl.BlockSpec(arr.shape, lambda i, j: (0, 0)))
        args.append(arr)
    for arr in epi_mods:
        in_specs.append(pl.BlockSpec((1, 1, tn), lambda i, j: (i // tiles_per_slab, 0, j)))
        args.append(arr)
    for arr in epi_rows:
        in_specs.append(pl.BlockSpec((tm, tn), lambda i, j: (i, j)))
        args.append(arr)
    for arr in epi_tabs:
        in_specs.append(pl.BlockSpec((tm, arr.shape[1]), lambda i, j: (i % tiles_per_slab, 0)))
        args.append(arr)
    out_specs, out_shapes = [], []
    for cols, dtype, bcols, cfn in outs:
        out_specs.append(pl.BlockSpec((tm, bcols), functools.partial(lambda i, j, cfn: (i, cfn(j)), cfn=cfn)))
        out_shapes.append(jax.ShapeDtypeStruct((m, cols), dtype))
    body = functools.partial(
        _fused_mm_body, n_x=len(xs), n_pc=len(pro_consts), n_pm=len(pro_mods), n_w=len(ws),
        n_ec=len(epi_cols), n_ek=len(epi_consts), n_em=len(epi_mods), n_er=len(epi_rows), n_et=len(epi_tabs),
        n_out=len(outs), prologue=prologue, epilogue=epilogue)
    res = pl.pallas_call(
        body, grid=grid, in_specs=in_specs, out_specs=out_specs, out_shape=out_shapes,
        scratch_shapes=[pltpu.VMEM((tm, k), BF16)],
        compiler_params=_cparams(("parallel", "arbitrary")), name=name)(*args)
    return res


def _pro_cast(xv, cv, mv):
    return xv[0]


def _pro_adaln(xv, cv, mv):
    return _rms(xv[0], cv[0]) * (1.0 + mv[1]) + mv[0]


def _pro_rms(xv, cv, mv):
    return _rms(xv[0], cv[0])


def _pro_ln_silu(xv, cv, mv):
    x = xv[0]
    mu = jnp.mean(x, axis=-1, keepdims=True)
    xc = x - mu
    var = jnp.mean(xc * xc, axis=-1, keepdims=True)
    return _silu((xc * lax.rsqrt(var + NORM_EPS)) * cv[0] + cv[1])


def _pro_s5_post(xv, cv, mv):
    return _gelu_tanh(xv[0] + cv[0] * xv[1])


def _epi_plain(accs, i, j, ec, em, er, et):
    return (accs[0],)


def _epi_bias_residual(accs, i, j, ec, em, er, et):
    return (er[0] + em[0] * (accs[0] + ec[0]),)


def _epi_residual(accs, i, j, ec, em, er, et):
    return (er[0] + em[0] * accs[0],)


def _epi_glu(accs, i, j, ec, em, er, et):
    return ((accs[0] + ec[0]) * jax.nn.sigmoid(accs[1] + ec[1]),)


def _epi_glu_residual(accs, i, j, ec, em, er, et):
    return (er[0] + em[0] * ((accs[0] + ec[0]) * jax.nn.sigmoid(accs[1] + ec[1])),)


def _modulation_body(c_ref, w_ref, b_ref, o_ref):
    h = _silu(c_ref[...]).astype(BF16)
    o_ref[0] = jnp.dot(h, w_ref[0].astype(BF16), preferred_element_type=F32) + b_ref[0]


def modulation_all(cond, ada_w, ada_b, tn):
    depth, d, n = ada_w.shape
    rows = cond.shape[0]
    return pl.pallas_call(
        _modulation_body, grid=(depth, n // tn),
        in_specs=[pl.BlockSpec((rows, d), lambda l, j: (0, 0)),
                  pl.BlockSpec((1, d, tn), lambda l, j: (l, 0, j)),
                  pl.BlockSpec((1, 1, tn), lambda l, j: (l, 0, j))],
        out_specs=pl.BlockSpec((1, rows, tn), lambda l, j: (l, 0, j)),
        out_shape=jax.ShapeDtypeStruct((depth, rows, n), F32),
        compiler_params=_cparams(("parallel", "parallel")), name="modulation")(
            cond, ada_w, ada_b.reshape(depth, 1, n))


def _norm_body(*refs, modulated):
    x_ref, g_ref = refs[0], refs[1]
    y = _rms(x_ref[...], g_ref[...])
    if modulated:
        shift_ref, scale_ref = refs[2], refs[3]
        y = y * (1.0 + scale_ref[0]) + shift_ref[0]
    o_ref = refs[-1]
    o_ref[...] = y.astype(o_ref.dtype)


def norm_rows(x, g, *, tm, slab, out_dtype, shift=None, scale=None, name="norm"):
    m, d = x.shape
    tiles_per_slab = slab // tm
    in_specs = [pl.BlockSpec((tm, d), lambda i: (i, 0)), pl.BlockSpec((1, d), lambda i: (0, 0))]
    args = [x, g]
    modulated = shift is not None
    if modulated:
        in_specs += [pl.BlockSpec((1, 1, d), lambda i: (i // tiles_per_slab, 0, 0))] * 2
        args += [shift, scale]
    return pl.pallas_call(
        functools.partial(_norm_body, modulated=modulated),
        grid=(m // tm,), in_specs=in_specs, out_specs=pl.BlockSpec((tm, d), lambda i: (i, 0)),
        out_shape=jax.ShapeDtypeStruct((m, d), out_dtype),
        compiler_params=_cparams(("parallel",)), name=name)(*args)


def _route_body(x_ref, g_ref, shift_ref, scale_ref, whi_ref, wlo_ref, b_ref, tri_ref,
                h_ref, idx_ref, gate_ref, cnt_ref, carry_ref, *, n_groups):
    @pl.when(pl.program_id(0) == 0)
    def _():
        carry_ref[...] = jnp.zeros(carry_ref.shape, F32)

    y = _rms(x_ref[...], g_ref[...]) * (1.0 + scale_ref[0]) + shift_ref[0]
    y_hi = y.astype(BF16)
    h_ref[...] = y_hi
    y_lo = (y - y_hi.astype(F32)).astype(BF16)
    nt = (((1,), (1,)), ((), ()))
    w_hi = whi_ref[...]
    logits = (lax.dot_general(w_hi, y_hi, nt, preferred_element_type=F32)
              + lax.dot_general(w_hi, y_lo, nt, preferred_element_type=F32)
              + lax.dot_general(wlo_ref[...], y_hi, nt, preferred_element_type=F32))
    n_exp, tm = logits.shape
    epg = n_exp // n_groups
    scores = jax.nn.sigmoid(logits)
    biased = scores + b_ref[...]
    sub = lax.broadcasted_iota(jnp.int32, (epg, tm), 0).astype(F32)
    best = sel = loc1 = loc2 = None
    for grp in range(n_groups):
        blk = biased[grp * epg:(grp + 1) * epg]
        m1 = jnp.max(blk, axis=0, keepdims=True)
        i1 = jnp.min(jnp.where(blk == m1, sub, float(epg)), axis=0, keepdims=True)
        rest = jnp.where(sub == i1, -jnp.inf, blk)
        m2 = jnp.max(rest, axis=0, keepdims=True)
        i2 = jnp.min(jnp.where(rest == m2, sub, float(epg)), axis=0, keepdims=True)
        gsum = m1 + m2
        if grp == 0:
            best, sel, loc1, loc2 = gsum, jnp.zeros_like(gsum), i1, i2
        else:
            better = gsum > best
            best = jnp.where(better, gsum, best)
            sel = jnp.where(better, float(grp), sel)
            loc1 = jnp.where(better, i1, loc1)
            loc2 = jnp.where(better, i2, loc2)
    e1 = sel * epg + loc1
    e2 = sel * epg + loc2
    row = lax.broadcasted_iota(jnp.int32, (n_exp, tm), 0).astype(F32)
    hit1, hit2 = row == e1, row == e2
    s1 = jnp.sum(jnp.where(hit1, scores, 0.0), axis=0, keepdims=True)
    s2 = jnp.sum(jnp.where(hit2, scores, 0.0), axis=0, keepdims=True)
    total = s1 + s2
    onehot = jnp.where(hit1, 1.0, 0.0) + jnp.where(hit2, 1.0, 0.0)
    csum = jnp.dot(onehot.astype(BF16), tri_ref[...], preferred_element_type=F32)
    before = carry_ref[...] + csum - onehot
    r1 = jnp.sum(jnp.where(hit1, before, 0.0), axis=0, keepdims=True)
    r2 = jnp.sum(jnp.where(hit2, before, 0.0), axis=0, keepdims=True)
    carry = carry_ref[...] + csum[:, tm - 1:tm]
    carry_ref[...] = carry
    cnt_ref[...] = jnp.broadcast_to(carry, cnt_ref.shape)
    idx_ref[...] = jnp.concatenate([e1, e2, r1, r2, jnp.zeros((4, tm), F32)], axis=0).astype(jnp.int32)
    gate_ref[...] = jnp.concatenate([s1 / total, s2 / total, jnp.zeros((6, tm), F32)], axis=0)


def route(x, g, shift, scale, wt_hi, wt_lo, b_router, *, tm, slab):
    m, d = x.shape
    n_exp = wt_hi.shape[0]
    tiles_per_slab = slab // tm
    tri = jnp.triu(jnp.ones((tm, tm), BF16))
    const = lambda i: (0, 0)
    return pl.pallas_call(
        functools.partial(_route_body, n_groups=N_EXPERT_GROUPS),
        grid=(m // tm,),
        in_specs=[pl.BlockSpec((tm, d), lambda i: (i, 0)), pl.BlockSpec((1, d), const),
                  pl.BlockSpec((1, 1, d), lambda i: (i // tiles_per_slab, 0, 0)),
                  pl.BlockSpec((1, 1, d), lambda i: (i // tiles_per_slab, 0, 0)),
                  pl.BlockSpec((n_exp, d), const), pl.BlockSpec((n_exp, d), const),
                  pl.BlockSpec((n_exp, 1), const), pl.BlockSpec((tm, tm), const)],
        out_specs=[pl.BlockSpec((tm, d), lambda i: (i, 0)), pl.BlockSpec((8, tm), lambda i: (0, i)),
                   pl.BlockSpec((8, tm), lambda i: (0, i)), pl.BlockSpec((n_exp, LANES), const)],
        out_shape=[jax.ShapeDtypeStruct((m, d), BF16), jax.ShapeDtypeStruct((8, m), jnp.int32),
                   jax.ShapeDtypeStruct((8, m), F32), jax.ShapeDtypeStruct((n_exp, LANES), F32)],
        scratch_shapes=[pltpu.VMEM((n_exp, 1), F32)],
        compiler_params=_cparams(("arbitrary",)), name="adaln_route")(
            x, g, shift, scale, wt_hi, wt_lo, b_router.reshape(n_exp, 1), tri)


def _mla_attn_body(*refs, nk, nh, aliased):
    if aliased:
        refs = refs[1:]
    qn_ref, qr_ref, kn_ref, vt_ref, kr_ref, o_ref, m_ref, l_ref, acc_ref = refs
    ki = pl.program_id(3)

    @pl.when(ki == 0)
    def _():
        m_ref[...] = jnp.full(m_ref.shape, -jnp.inf, F32)
        l_ref[...] = jnp.zeros(l_ref.shape, F32)
        acc_ref[...] = jnp.zeros(acc_ref.shape, F32)

    kr = kr_ref[...]
    scores, probs = {}, {}

    def qk(hh):
        q = jnp.concatenate([qn_ref[:, hh * MLA_NOPE:(hh + 1) * MLA_NOPE],
                             qr_ref[:, hh * MLA_ROPE:(hh + 1) * MLA_ROPE]], axis=1)
        k = jnp.concatenate([kn_ref[:, hh * MLA_NOPE:(hh + 1) * MLA_NOPE], kr], axis=1)
        scores[hh] = lax.dot_general(k, q, (((1,), (1,)), ((), ())), preferred_element_type=F32)

    def softmax(hh):
        st = scores.pop(hh)
        m_prev = m_ref[hh]
        m_new = jnp.maximum(m_prev, jnp.max(st, axis=0, keepdims=True))
        alpha = jnp.exp2(m_prev - m_new)
        p = jnp.exp2(st - m_new)
        l_ref[hh] = alpha * l_ref[hh] + jnp.sum(p, axis=0, keepdims=True)
        m_ref[hh] = m_new
        probs[hh] = (alpha, p.astype(BF16))

    def pv(hh):
        alpha, p = probs.pop(hh)
        acc_ref[hh] = alpha * acc_ref[hh] + jnp.dot(
            vt_ref[hh * MLA_V:(hh + 1) * MLA_V, :], p, preferred_element_type=F32)

    for t in range(nh + 2):
        if t < nh:
            qk(t)
        if 0 <= t - 1 < nh:
            softmax(t - 1)
        if 0 <= t - 2 < nh:
            pv(t - 2)

    @pl.when(ki == nk - 1)
    def _():
        o_ref[...] = jnp.concatenate([(acc_ref[hh] / l_ref[hh]).T for hh in range(nh)],
                                     axis=1).astype(o_ref.dtype)


def mla_attention(qn, qr, kn, vt, kr, *, n_heads, batch, nq_rows, nk_rows, q_row0, k_row0, tq, tk, prev_out=None):
    t = qn.shape[0]
    nh = min(4, n_heads)
    hg = n_heads // nh
    nq, nk = nq_rows // tq, nk_rows // tk
    assert q_row0 % tq == 0 and k_row0 % tk == 0 and nq_rows % tq == 0 and nk_rows % tk == 0
    qb0, kb0 = q_row0 // tq, k_row0 // tk

    def qmap(b, h, qi, ki):
        return (qb0 + b * nq + qi, h)

    def kmap(b, h, qi, ki):
        return (kb0 + b * nk + ki, h)

    in_specs = [pl.BlockSpec((tq, nh * MLA_NOPE), qmap),
                pl.BlockSpec((tq, nh * MLA_ROPE), qmap),
                pl.BlockSpec((tk, nh * MLA_NOPE), kmap),
                pl.BlockSpec((nh * MLA_V, tk), lambda b, h, qi, ki: (h, kb0 + b * nk + ki)),
                pl.BlockSpec((tk, MLA_ROPE), lambda b, h, qi, ki: (kb0 + b * nk + ki, 0))]
    args = [qn, qr, kn, vt, kr]
    aliases = {}
    if prev_out is not None:
        in_specs = [pl.BlockSpec(memory_space=pl.ANY)] + in_specs
        args = [prev_out] + args
        aliases = {0: 0}
    return pl.pallas_call(
        functools.partial(_mla_attn_body, nk=nk, nh=nh, aliased=prev_out is not None),
        grid=(batch, hg, nq, nk), in_specs=in_specs,
        out_specs=pl.BlockSpec((tq, nh * MLA_V), qmap),
        out_shape=jax.ShapeDtypeStruct((t, n_heads * MLA_V), BF16),
        scratch_shapes=[pltpu.VMEM((nh, 1, tq), F32), pltpu.VMEM((nh, 1, tq), F32),
                        pltpu.VMEM((nh, MLA_V, tq), F32)],
        input_output_aliases=aliases,
        compiler_params=_cparams(("parallel", "parallel", "parallel", "arbitrary")),
        name="mla_attention")(*args)


def _mm_nt_body(w_ref, x_ref, o_ref):
    o_ref[...] = lax.dot_general(w_ref[...], x_ref[...], (((1,), (1,)), ((), ())),
                                 preferred_element_type=F32).astype(o_ref.dtype)


def mm_nt(wt, x, *, tm, name):
    n, k = wt.shape
    m = x.shape[0]
    return pl.pallas_call(
        _mm_nt_body, grid=(m // tm,),
        in_specs=[pl.BlockSpec((n, k), lambda i: (0, 0)), pl.BlockSpec((tm, k), lambda i: (i, 0))],
        out_specs=pl.BlockSpec((n, tm), lambda i: (0, i)),
        out_shape=jax.ShapeDtypeStruct((n, m), BF16),
        compiler_params=_cparams(("parallel",)), name=name)(wt, x)


def _gqa_attn_body(*refs, scale, n_seg, window, nq, aliased, group):
    if aliased:
        refs = refs[1:]
    sink_ref, q_ref = refs[0], refs[1]
    k_refs = refs[2:2 + n_seg]
    v_refs = refs[2 + n_seg:2 + 2 * n_seg]
    o_ref = refs[2 + 2 * n_seg]
    p_idx = pl.program_id(1)
    qi = pl.program_id(2)
    d = GQA_HEAD_DIM
    q = q_ref[...]
    tq = q.shape[0]
    ks = [r[...] for r in k_refs]
    vs = [r[...] for r in v_refs]
    if ks[-1].ndim == 3:
        ks[-1] = ks[-1][0]
        vs[-1] = vs[-1][0]
    n_keys = sum(kk.shape[0] for kk in ks)
    if window:
        row = lax.broadcasted_iota(jnp.int32, (group * tq, n_keys), 0) % tq
        col = lax.broadcasted_iota(jnp.int32, (group * tq, n_keys), 1)
        n_loc = 3 * BAND_BLOCK
        in_band = jnp.abs(row + BAND_BLOCK - col) <= WINDOW
        lo = jnp.where(qi > 0, 0, BAND_BLOCK)
        hi = jnp.where(qi < nq - 1, n_loc, 2 * BAND_BLOCK)
        valid = (col >= n_loc) | (in_band & (col >= lo) & (col < hi))
    outs = []
    for kvh in range(2):
        k = jnp.concatenate([kk[:, kvh * d:(kvh + 1) * d] for kk in ks], axis=0)
        v = jnp.concatenate([vv[:, kvh * d:(kvh + 1) * d] for vv in vs], axis=0)
        base = kvh * group * d
        q4 = jnp.concatenate([q[:, base + g * d: base + (g + 1) * d] for g in range(group)], axis=0)
        s = lax.dot_general(q4, k, (((1,), (1,)), ((), ())), preferred_element_type=F32) * scale
        if window:
            s = jnp.where(valid, s, NEG_INF)
        head0 = (p_idx * 2 + kvh) * group
        snk = jnp.concatenate([jnp.full((tq, 1), sink_ref[head0 + g], F32) for g in range(group)], axis=0)
        m = jnp.maximum(jnp.max(s, axis=1, keepdims=True), snk)
        p = jnp.exp(s - m)
        denom = jnp.sum(p, axis=1, keepdims=True) + jnp.exp(snk - m)
        o4 = jnp.dot(p.astype(BF16), v, preferred_element_type=F32) / denom
        outs += [o4[g * tq:(g + 1) * tq] for g in range(group)]
    o_ref[...] = jnp.concatenate(outs, axis=1).astype(o_ref.dtype)


def gqa_attention(qkv, sink, *, batch, n_rows, q_row0, window, ctx_k=None, ctx_v=None, prev_out=None):
    t = qkv.shape[0]
    d, h, kvh = GQA_HEAD_DIM, GQA_HEADS, GQA_KV_HEADS
    group = h // kvh
    pairs = kvh // 2
    qw = 2 * group * d
    kcol0 = (h * d) // LANES
    vcol0 = (h * d + kvh * d) // LANES
    tq = BAND_BLOCK if window else n_rows
    nq = n_rows // tq
    qb0 = q_row0 // tq
    assert q_row0 % tq == 0

    def qmap(b, p, i):
        return (qb0 + b * nq + i, p)

    in_specs = [pl.BlockSpec(memory_space=pltpu.SMEM), pl.BlockSpec((tq, qw), qmap)]
    args = [sink, qkv]
    if window:
        def kv_specs(col0):
            return [
                pl.BlockSpec((tq, LANES), lambda b, p, i: (qb0 + b * nq + jnp.maximum(i - 1, 0), col0 + p)),
                pl.BlockSpec((tq, LANES), lambda b, p, i: (qb0 + b * nq + i, col0 + p)),
                pl.BlockSpec((tq, LANES), lambda b, p, i: (qb0 + b * nq + jnp.minimum(i + 1, nq - 1), col0 + p)),
            ]
        n_ctx = ctx_k.shape[1]
        ctx_spec = pl.BlockSpec((1, n_ctx, LANES), lambda b, p, i: (b, 0, p))
        in_specs += kv_specs(kcol0) + [ctx_spec] + kv_specs(vcol0) + [ctx_spec]
        args += [qkv, qkv, qkv, ctx_k, qkv, qkv, qkv, ctx_v]
        n_seg = 4
    else:
        in_specs += [pl.BlockSpec((tq, LANES), lambda b, p, i: (qb0 + b * nq + i, kcol0 + p)),
                     pl.BlockSpec((tq, LANES), lambda b, p, i: (qb0 + b * nq + i, vcol0 + p))]
        args += [qkv, qkv]
        n_seg = 1
    aliases = {}
    if prev_out is not None:
        in_specs = [pl.BlockSpec(memory_space=pl.ANY)] + in_specs
        args = [prev_out] + args
        aliases = {0: 0}
    return pl.pallas_call(
        functools.partial(_gqa_attn_body, scale=d ** -0.5, n_seg=n_seg, window=window, nq=nq,
                          aliased=prev_out is not None, group=group),
        grid=(batch, pairs, nq), in_specs=in_specs,
        out_specs=pl.BlockSpec((tq, qw), qmap),
        out_shape=jax.ShapeDtypeStruct((t, h * d), BF16),
        input_output_aliases=aliases,
        compiler_params=_cparams(("parallel", "parallel", "parallel")),
        name="gqa_window_attention" if window else "gqa_dense_attention")(*args)


def _s5_body(u_ref, w1_ref, w2_ref, coef_ref, h0_ref, y_ref, fin_ref, z_ref, sp_ref, *, gb, nb, n_steps):
    lc = S5_SCAN_CHUNK * S5_GROUP_CH
    ns = 2 * S5_STATE
    for g in range(gb):
        z_ref[g] = jnp.dot(u_ref[g], w1_ref[g], preferred_element_type=F32)

    def cmul_add(coef, g, dirn, s, z):
        a1 = coef[g, 2 * dirn:2 * dirn + 1, :]
        a2 = coef[g, 2 * dirn + 1:2 * dirn + 2, :]
        return a1 * s + a2 * pltpu.roll(s, S5_STATE, 1) + z

    coef = coef_ref[...]

    def step(c, carry):
        new = []
        rf = pl.multiple_of(c * nb, nb)
        rb = pl.multiple_of((n_steps - 1 - c) * nb, nb)
        for g in range(gb):
            sf, sb = carry[2 * g], carry[2 * g + 1]
            sp_ref[g, pl.ds(rf, nb), 0:ns] = sf
            sp_ref[g, pl.ds(rb, nb), ns:2 * ns] = sb
            new.append(cmul_add(coef, g, 0, sf, z_ref[g, pl.ds(rf, nb), 0:ns]))
            new.append(cmul_add(coef, g, 1, sb, z_ref[g, pl.ds(rb, nb), ns:2 * ns]))
        return tuple(new)

    init = []
    for g in range(gb):
        init += [h0_ref[g, 0], h0_ref[g, 1]]
    fin = lax.fori_loop(0, n_steps, step, tuple(init))
    for g in range(gb):
        fin_ref[g, 0] = fin[2 * g]
        fin_ref[g, 1] = fin[2 * g + 1]
        y_ref[g] = (jnp.dot(u_ref[g], w2_ref[g, 0:lc, :], preferred_element_type=F32)
                    + jnp.dot(sp_ref[g].astype(BF16), w2_ref[g, lc:, :], preferred_element_type=F32))


def s5_scan(u, w1, w2, coef, h0, *, gb, nb):
    g, cols, lc = u.shape
    n_steps = cols // nb
    ns2 = 4 * S5_STATE
    return pl.pallas_call(
        functools.partial(_s5_body, gb=gb, nb=nb, n_steps=n_steps),
        grid=(g // gb,),
        in_specs=[pl.BlockSpec((gb, cols, lc), lambda i: (i, 0, 0)),
                  pl.BlockSpec((gb, lc, ns2), lambda i: (i, 0, 0)),
                  pl.BlockSpec((gb, lc + ns2, lc), lambda i: (i, 0, 0)),
                  pl.BlockSpec((gb, 4, 2 * S5_STATE), lambda i: (i, 0, 0)),
                  pl.BlockSpec((gb, 2, nb, 2 * S5_STATE), lambda i: (i, 0, 0, 0))],
        out_specs=[pl.BlockSpec((gb, cols, lc), lambda i: (i, 0, 0)),
                   pl.BlockSpec((gb, 2, nb, 2 * S5_STATE), lambda i: (i, 0, 0, 0))],
        out_shape=[jax.ShapeDtypeStruct((g, cols, lc), F32),
                   jax.ShapeDtypeStruct((g, 2, nb, 2 * S5_STATE), F32)],
        scratch_shapes=[pltpu.VMEM((gb, cols, ns2), F32), pltpu.VMEM((gb, cols, ns2), F32)],
        compiler_params=_cparams(("parallel",)), name="s5_scan")(u, w1, w2, coef, h0)


def s5_weights(lam_re, lam_im, b_re, b_im, c_re, c_im, log_dt):
    hp = lax.Precision.HIGHEST
    L, H, P = S5_SCAN_CHUNK, S5_GROUP_CH, S5_STATE
    G = lam_re.shape[1]
    dt = jnp.exp(log_dt)[..., None]
    lr, li = lam_re, lam_im
    mag = jnp.exp(lr * dt)
    ar, ai = mag * jnp.cos(li * dt), mag * jnp.sin(li * dt)
    den = lr * lr + li * li
    cr = ((ar - 1.0) * lr + ai * li) / den
    ci = (ai * lr - (ar - 1.0) * li) / den
    bbr = cr[..., None] * b_re - ci[..., None] * b_im
    bbi = cr[..., None] * b_im + ci[..., None] * b_re
    kk = jnp.arange(L + 1, dtype=F32)[:, None, None, None]
    pmag = jnp.exp(kk * (lr * dt)[None])
    pr, pi = pmag * jnp.cos(kk * (li * dt)[None]), pmag * jnp.sin(kk * (li * dt)[None])
    car = c_re[None] * pr[:, :, :, None, :] - c_im[None] * pi[:, :, :, None, :]
    cai = c_re[None] * pi[:, :, :, None, :] + c_im[None] * pr[:, :, :, None, :]
    mker = (jnp.einsum('ldgop,dgpi->ldgoi', car[:L], bbr, precision=hp)
            - jnp.einsum('ldgop,dgpi->ldgoi', cai[:L], bbi, precision=hp))
    s_idx = jnp.arange(L)[:, None]
    j_idx = jnp.arange(L)[None, :]

    def toeplitz(m, lag, mask):
        t = m[jnp.clip(lag, 0, L - 1)] * mask[:, :, None, None, None]
        return t.transpose(2, 0, 4, 1, 3)

    t_all = (toeplitz(mker[:, 0], j_idx - s_idx, (j_idx >= s_idx).astype(F32))
             + toeplitz(mker[:, 1], s_idx - j_idx, (s_idx >= j_idx).astype(F32))).reshape(G, L * H, L * H)

    def bc(dirn, powers):
        qr_, qi_ = pr[powers, dirn], pi[powers, dirn]
        re = qr_[..., None] * bbr[dirn][None] - qi_[..., None] * bbi[dirn][None]
        im = qr_[..., None] * bbi[dirn][None] + qi_[..., None] * bbr[dirn][None]
        return jnp.concatenate([re, im], axis=2).transpose(1, 0, 3, 2).reshape(G, L * H, 2 * P)

    def cc(dirn, powers):
        wr = car[powers, dirn]
        wi = cai[powers, dirn]
        return jnp.concatenate([wr, -wi], axis=3).transpose(1, 3, 0, 2).reshape(G, 2 * P, L * H)

    ar_l = jnp.arange(L)
    w1 = jnp.concatenate([bc(0, L - 1 - ar_l), bc(1, ar_l)], axis=2)
    w2 = jnp.concatenate([t_all, cc(0, ar_l + 1), cc(1, L - ar_l)], axis=1)
    coef = jnp.stack([jnp.concatenate([pr[L, 0], pr[L, 0]], -1), jnp.concatenate([-pi[L, 0], pi[L, 0]], -1),
                      jnp.concatenate([pr[L, 1], pr[L, 1]], -1), jnp.concatenate([-pi[L, 1], pi[L, 1]], -1)], axis=1)
    return w1.astype(BF16), w2.astype(BF16), coef


def _dwconv_body(cur_ref, prev_ref, next_ref, w_ref, b_ref, o_ref, zp_ref, *, tile, slab, seq_prompt, rows_sub):
    t = pl.program_id(0)
    row0 = t * tile
    seq_len = jnp.where(row0 < slab, seq_prompt, slab)
    pos = row0 % seq_len
    has_prev = pos != 0
    has_next = pos + tile != seq_len
    halo = CONV_HALO
    zp_ref[0:halo, :] = jnp.where(has_prev, prev_ref[...], 0.0)
    zp_ref[halo:halo + tile, :] = cur_ref[...]
    zp_ref[halo + tile:, :] = jnp.where(has_next, next_ref[...], 0.0)
    w = w_ref[...]
    off = halo - CONV_WIDTH // 2
    for r0 in range(0, tile, rows_sub):
        acc = jnp.zeros((rows_sub, w.shape[1]), F32) + b_ref[...]
        for kk in range(CONV_WIDTH):
            acc = acc + w[kk:kk + 1, :] * zp_ref[r0 + off + kk:r0 + off + kk + rows_sub, :]
        o_ref[r0:r0 + rows_sub, :] = acc


def dwconv(z, w, b, *, tile, cb, slab, seq_prompt, rows_sub=32):
    m, d = z.shape
    halo = CONV_HALO
    hb = tile // halo
    n_halo_blocks = m // halo
    assert seq_prompt % tile == 0 and slab % tile == 0 and tile % halo == 0
    return pl.pallas_call(
        functools.partial(_dwconv_body, tile=tile, slab=slab, seq_prompt=seq_prompt, rows_sub=rows_sub),
        grid=(m // tile, d // cb),
        in_specs=[pl.BlockSpec((tile, cb), lambda t, c: (t, c)),
                  pl.BlockSpec((halo, cb), lambda t, c: (jnp.maximum(t * hb - 1, 0), c)),
                  pl.BlockSpec((halo, cb), lambda t, c: (jnp.minimum((t + 1) * hb, n_halo_blocks - 1), c)),
                  pl.BlockSpec((CONV_WIDTH, cb), lambda t, c: (0, c)),
                  pl.BlockSpec((1, cb), lambda t, c: (0, c))],
        out_specs=pl.BlockSpec((tile, cb), lambda t, c: (t, c)),
        out_shape=jax.ShapeDtypeStruct((m, d), F32),
        scratch_shapes=[pltpu.VMEM((tile + 2 * halo, cb), F32)],
        compiler_params=_cparams(("parallel", "parallel")), name="dwconv")(z, z, z, w, b)


def _moe_body(be_ref, nu_ref, x_ref, w1_ref, w3_ref, w2_ref, o_ref, w1s, w3s, w2s):
    b = pl.program_id(0)
    prev = be_ref[jnp.maximum(b - 1, 0)]

    @pl.when((b == 0) | (be_ref[b] != prev))
    def _():
        w1s[...] = w1_ref[0].astype(BF16)
        w3s[...] = w3_ref[0].astype(BF16)
        w2s[...] = w2_ref[0].astype(BF16)

    @pl.when(b < nu_ref[0])
    def _():
        x = x_ref[...]
        a = jnp.dot(x, w1s[...], preferred_element_type=F32)
        g = jnp.dot(x, w3s[...], preferred_element_type=F32)
        o_ref[...] = jnp.dot((_silu(a) * g).astype(BF16), w2s[...], preferred_element_type=F32)

    @pl.when(b >= nu_ref[0])
    def _():
        o_ref[...] = jnp.zeros(o_ref.shape, o_ref.dtype)


def moe_experts(xg, block_expert, n_used, w1, w3, w2, *, tm):
    mp, d = xg.shape
    e, _, f = w1.shape
    grid_spec = pltpu.PrefetchScalarGridSpec(
        num_scalar_prefetch=2, grid=(mp // tm,),
        in_specs=[pl.BlockSpec((tm, d), lambda b, be, nu: (b, 0)),
                  pl.BlockSpec((1, d, f), lambda b, be, nu: (be[b], 0, 0)),
                  pl.BlockSpec((1, d, f), lambda b, be, nu: (be[b], 0, 0)),
                  pl.BlockSpec((1, f, d), lambda b, be, nu: (be[b], 0, 0))],
        out_specs=pl.BlockSpec((tm, d), lambda b, be, nu: (b, 0)),
        scratch_shapes=[pltpu.VMEM((d, f), BF16), pltpu.VMEM((d, f), BF16), pltpu.VMEM((f, d), BF16)])
    return pl.pallas_call(
        _moe_body, grid_spec=grid_spec, out_shape=jax.ShapeDtypeStruct((mp, d), F32),
        compiler_params=_cparams(("arbitrary",)), name="moe_experts")(block_expert, n_used, xg, w1, w3, w2)


def _combine_body(x_ref, y1_ref, y2_ref, g_ref, gm_ref, o_ref):
    g = g_ref[...]
    o_ref[...] = x_ref[...] + gm_ref[0] * (g[:, 0:1] * y1_ref[...] + g[:, 1:2] * y2_ref[...])


def moe_combine(x, y1, y2, gates, gate_mod, *, tm, slab):
    m, d = x.shape
    tiles_per_slab = slab // tm
    row = pl.BlockSpec((tm, d), lambda i: (i, 0))
    return pl.pallas_call(
        _combine_body, grid=(m // tm,),
        in_specs=[row, row, row, pl.BlockSpec((tm, LANES), lambda i: (i, 0)),
                  pl.BlockSpec((1, 1, d), lambda i: (i // tiles_per_slab, 0, 0))],
        out_specs=row, out_shape=jax.ShapeDtypeStruct((m, d), F32),
        compiler_params=_cparams(("parallel",)), name="moe_combine")(x, y1, y2, gates, gate_mod)


def moe_layer(x, norm_g, shift, scale, gate_mod, router_w, b_router, w1, w3, w2, *, tm_rows, tm_moe, slab):
    t, d = x.shape
    n_exp = w1.shape[0]
    hb, idx, gate8, cnt = route(x, norm_g, shift, scale, router_w[0], router_w[1], b_router, tm=tm_rows, slab=slab)
    expert = idx[0:TOP_K].T
    rank = idx[TOP_K:2 * TOP_K].T
    gate = gate8[0:TOP_K].T
    counts = cnt[:, 0].astype(jnp.int32)
    padded = (counts + tm_moe - 1) // tm_moe * tm_moe
    pends = jnp.cumsum(padded)
    pstarts = pends - padded
    dest = (pstarts[expert] + rank).astype(jnp.int32).reshape(-1)
    n_assign = t * TOP_K
    n_blocks = (n_assign + n_exp * (tm_moe - 1) + tm_moe - 1) // tm_moe
    block_expert = jnp.minimum(
        jnp.searchsorted(pends, jnp.arange(n_blocks, dtype=jnp.int32) * tm_moe, side='right'),
        n_exp - 1).astype(jnp.int32)
    n_used = (pends[-1] // tm_moe).astype(jnp.int32).reshape(1)
    tok = jnp.arange(n_assign, dtype=jnp.int32) // TOP_K
    src = jnp.zeros((n_blocks * tm_moe,), jnp.int32).at[dest].set(tok)
    xg = jnp.take(hb, src, axis=0)
    yg = moe_experts(xg, block_expert, n_used, w1, w3, w2, tm=tm_moe)
    dest2 = dest.reshape(t, TOP_K)
    y1 = jnp.take(yg, dest2[:, 0], axis=0)
    y2 = jnp.take(yg, dest2[:, 1], axis=0)
    gates = jnp.pad(gate, ((0, 0), (0, LANES - TOP_K)))
    return moe_combine(x, y1, y2, gates, gate_mod, tm=tm_rows, slab=slab)


def _rope_tables(n_tokens):
    rows = n_tokens // GRID_W
    r = jnp.repeat(jnp.arange(rows), GRID_W).astype(F32)
    col = jnp.tile(jnp.arange(GRID_W), rows).astype(F32)
    n_freq = 64 // 4
    inv = ROPE_BASE ** (-jnp.arange(n_freq, dtype=F32) / n_freq)
    ang = jnp.concatenate([r[:, None] * inv, col[:, None] * inv], axis=-1)
    c, s = jnp.cos(ang), jnp.sin(ang)
    return jnp.concatenate([c, c, c, c], axis=1), jnp.concatenate([-s, s, -s, s], axis=1)


def _pick_tile(n, pref):
    t = pref
    while n % t:
        t //= 2
    return t


def kernel(x_prompt, x_sample, cache_mla_ckv, cache_mla_krope, cache_gqa_k, cache_gqa_v, state_s5, c, c_ctx, norm_mix_g, norm_ffn_g, ada_w, ada_b, final_norm_g, mla_wq_a, mla_q_norm, mla_wq_b, mla_wkv_a, mla_kv_norm, mla_wkv_b, mla_wo, gqa_wq, gqa_wk, gqa_wv, gqa_wo, gqa_sink, s5_lam_re, s5_lam_im, s5_b_re, s5_b_im, s5_c_re, s5_c_im, s5_log_dt, s5_d, s5_w_glu, s5_b_glu, conv_w_pw1, conv_b_pw1, conv_w_dw, conv_b_dw, conv_ln_g, conv_ln_b, conv_w_pw2, conv_b_pw2, moe_w_router, moe_b_router, moe_w1, moe_w3, moe_w2):
    bp, sp, d = x_prompt.shape
    bs, ns, _ = x_sample.shape
    depth = ada_w.shape[0]
    past = cache_mla_ckv.shape[2]
    slab = ns
    assert bp * sp == slab, "prompt rows must fill exactly one slab"
    n_slab = 1 + bs
    t = n_slab * slab
    tm = _pick_tile(slab, 1024)
    tmh = _pick_tile(slab, 512)
    tn = _pick_tile(d, 512)
    x = jnp.concatenate([x_prompt.reshape(slab, d), x_sample.reshape(bs * slab, d)], axis=0)

    rows_pad = -(-n_slab // 16) * 16
    cond = jnp.concatenate([c_ctx[None], c, jnp.zeros((rows_pad - n_slab, d), F32)], axis=0)
    mod = modulation_all(cond, ada_w, ada_b, _pick_tile(6 * d, 1024))
    mod = mod[:, :n_slab].reshape(depth, n_slab, 6, 1, d).transpose(0, 2, 1, 3, 4)

    cos_t, sin_t = _rope_tables(ns)
    w_r = moe_w_router.T
    w_r_hi = w_r.astype(BF16)
    w_r_lo = (w_r - w_r_hi.astype(F32)).astype(BF16)
    n_mixers = 4
    outs = {}

    for layer in range(depth):
        kind, r = layer % n_mixers, layer // n_mixers
        sh_mix, sc_mix, gt_mix, sh_ffn, sc_ffn, gt_ffn = [mod[layer, q] for q in range(6)]
        g_mix = norm_mix_g[layer][None]
        if kind == 0:
            qrank = mla_wq_a.shape[2]
            kvrank = mla_kv_norm.shape[1]
            hds = MLA_HEADS
            wa = jnp.concatenate([mla_wq_a[r], mla_wkv_a[r],
                                  jnp.zeros((d, LANES - MLA_ROPE), F32)], axis=1).astype(BF16)
            na = wa.shape[1]
            q_norm, kv_norm = mla_q_norm[r][None], mla_kv_norm[r][None]

            def epi_a(accs, i, j, ec, em, er, et, qrank=qrank, kvrank=kvrank, tiles=slab // tmh):
                a = accs[0]
                qn = _rms(a[:, :qrank], ec[0])
                ckv = _rms(a[:, qrank:qrank + kvrank], ec[1])
                kr = a[:, qrank + kvrank:]
                kr = jnp.where(i >= tiles, _rope_rot(kr, et[0], et[1]), kr)
                return qn, ckv, kr

            qn, ckv, kr = fused_mm(
                name="mla_proj_a", m=t, k=d, n=na, tm=tmh, tn=na, slab=slab,
                xs=[(x, 0)], pro_consts=[g_mix], pro_mods=[sh_mix, sc_mix], prologue=_pro_adaln,
                ws=[(wa, 0)], epi_tabs=[cos_t, sin_t], epi_consts=[q_norm, kv_norm], epilogue=epi_a,
                outs=[(qrank, BF16, qrank, lambda j: 0), (kvrank, F32, kvrank, lambda j: 0),
                      (LANES, F32, LANES, lambda j: 0)])
            outs['ckv'] = ckv[:slab].reshape(bp, 1, sp, kvrank)
            outs['krope'] = kr[:slab, :MLA_ROPE].reshape(bp, 1, sp, MLA_ROPE)
            wqb = mla_wq_b[r].reshape(qrank, hds, MLA_NOPE + MLA_ROPE)
            wqb_n = wqb[:, :, :MLA_NOPE].reshape(qrank, hds * MLA_NOPE).astype(BF16)
            wqb_r = wqb[:, :, MLA_NOPE:].reshape(qrank, hds * MLA_ROPE).astype(BF16)
            qscale = (MLA_NOPE + MLA_ROPE) ** -0.5 * math.log2(math.e)

            def epi_qn(accs, i, j, ec, em, er, et, qscale=qscale):
                return (accs[0] * qscale,)

            (q_nope,) = fused_mm(
                name="mla_q_nope", m=t, k=qrank, n=hds * MLA_NOPE, tm=tm, tn=_pick_tile(hds * MLA_NOPE, 1024),
                slab=slab, xs=[(qn, 0)], prologue=_pro_cast, ws=[(wqb_n, 0)], epilogue=epi_qn,
                outs=[(hds * MLA_NOPE, BF16, _pick_tile(hds * MLA_NOPE, 1024), lambda j: j)])

            def epi_qr(accs, i, j, ec, em, er, et, tiles=slab // tm, qscale=qscale):
                a = accs[0]
                return (jnp.where(i >= tiles, _rope_rot(a, et[0], et[1]), a) * qscale,)

            (q_rope,) = fused_mm(
                name="mla_q_rope", m=t, k=qrank, n=hds * MLA_ROPE, tm=tm, tn=hds * MLA_ROPE,
                slab=slab, xs=[(qn, 0)], prologue=_pro_cast, ws=[(wqb_r, 0)], epilogue=epi_qr,
                epi_tabs=[cos_t, sin_t], outs=[(hds * MLA_ROPE, BF16, hds * MLA_ROPE, lambda j: 0)])
            ckv_s = ckv[slab:].reshape(bs, ns, kvrank)
            kr_s = kr[slab:, :MLA_ROPE].reshape(bs, ns, MLA_ROPE)
            ckv_all = jnp.concatenate(
                [jnp.concatenate([ckv_s, cache_mla_ckv[:, r]], axis=1).reshape(-1, kvrank), ckv[:slab]],
                axis=0).astype(BF16)
            kr_all = jnp.concatenate(
                [jnp.concatenate([kr_s, cache_mla_krope[:, r]], axis=1).reshape(-1, MLA_ROPE),
                 kr[:slab, :MLA_ROPE]], axis=0).astype(BF16)
            wkvb = mla_wkv_b[r].reshape(kvrank, hds, MLA_NOPE + MLA_V)
            wkvb_k = wkvb[:, :, :MLA_NOPE].reshape(kvrank, hds * MLA_NOPE).astype(BF16)
            wkvb_vt = wkvb[:, :, MLA_NOPE:].reshape(kvrank, hds * MLA_V).T.astype(BF16)
            tk_rows = ckv_all.shape[0]
            tm_kv = _pick_tile(tk_rows, 1024)
            (k_nope,) = fused_mm(
                name="mla_k_expand", m=tk_rows, k=kvrank, n=hds * MLA_NOPE, tm=tm_kv,
                tn=_pick_tile(hds * MLA_NOPE, 1024), slab=tm_kv, xs=[(ckv_all, 0)], prologue=_pro_cast,
                ws=[(wkvb_k, 0)], epilogue=_epi_plain,
                outs=[(hds * MLA_NOPE, BF16, _pick_tile(hds * MLA_NOPE, 1024), lambda j: j)])
            v_t = mm_nt(wkvb_vt, ckv_all, tm=_pick_tile(tk_rows, 512), name="mla_v_expand_t")
            k_prompt0 = bs * (ns + past)
            att = mla_attention(q_nope, q_rope, k_nope, v_t, kr_all, n_heads=hds, batch=bp, nq_rows=sp,
                                nk_rows=sp, q_row0=0, k_row0=k_prompt0, tq=sp, tk=sp)
            tqs = _pick_tile(ns, 512)
            tks = next(c for c in (1536, 1152, 1024, 768, 512, 384, 256, 128) if (ns + past) % c == 0)
            att = mla_attention(q_nope, q_rope, k_nope, v_t, kr_all, n_heads=hds, batch=bs, nq_rows=ns,
                                nk_rows=ns + past, q_row0=slab, k_row0=0, tq=tqs, tk=tks, prev_out=att)
            (x,) = fused_mm(
                name="mla_out_proj", m=t, k=hds * MLA_V, n=d, tm=tm, tn=tn, slab=slab,
                xs=[(att, 0)], prologue=_pro_cast, ws=[(mla_wo[r].astype(BF16), 0)],
                epi_mods=[gt_mix], epi_rows=[x], epilogue=_epi_residual,
                outs=[(d, F32, tn, lambda j: j)])
        elif kind == 1:
            hq, hkv, hd = GQA_HEADS, GQA_KV_HEADS, GQA_HEAD_DIM
            wqkv = jnp.concatenate([gqa_wq[r], gqa_wk[r], gqa_wv[r]], axis=1).astype(BF16)
            nqkv = wqkv.shape[1]
            tn_qkv = hkv * hd
            n_rope_tiles = (hq * hd + hkv * hd) // tn_qkv
            n_q_tiles = (hq * hd) // tn_qkv

            def epi_qkv(accs, i, j, ec, em, er, et, tiles=slab // tmh, n_rope_tiles=n_rope_tiles):
                a = accs[0]
                roped = jnp.where((i >= tiles) & (j < n_rope_tiles), _rope_rot(a, et[0], et[1]), a)
                return roped, a

            qkv, kv_f32 = fused_mm(
                name="gqa_qkv_proj", m=t, k=d, n=nqkv, tm=tmh, tn=tn_qkv, slab=slab,
                xs=[(x, 0)], pro_consts=[g_mix], pro_mods=[sh_mix, sc_mix], prologue=_pro_adaln,
                ws=[(wqkv, 0)], epi_tabs=[cos_t, sin_t], epilogue=epi_qkv,
                outs=[(nqkv, BF16, tn_qkv, lambda j: j),
                      (2 * tn_qkv, F32, tn_qkv, functools.partial(lambda j, nq: jnp.maximum(j - nq, 0), nq=n_q_tiles))])
            outs['gqa_k'] = kv_f32[:slab, :hkv * hd].reshape(bp, 1, sp, hkv, hd)
            outs['gqa_v'] = kv_f32[:slab, hkv * hd:].reshape(bp, 1, sp, hkv, hd)
            att = gqa_attention(qkv, gqa_sink[r], batch=bp, n_rows=sp, q_row0=0, window=False)
            ctx_k = cache_gqa_k[:, r].reshape(bs, past, hkv * hd).astype(BF16)
            ctx_v = cache_gqa_v[:, r].reshape(bs, past, hkv * hd).astype(BF16)
            att = gqa_attention(qkv, gqa_sink[r], batch=bs, n_rows=ns, q_row0=slab, window=True,
                                ctx_k=ctx_k, ctx_v=ctx_v, prev_out=att)
            (x,) = fused_mm(
                name="gqa_out_proj", m=t, k=hq * hd, n=d, tm=tm, tn=tn, slab=slab,
                xs=[(att, 0)], prologue=_pro_cast, ws=[(gqa_wo[r].astype(BF16), 0)],
                epi_mods=[gt_mix], epi_rows=[x], epilogue=_epi_residual,
                outs=[(d, F32, tn, lambda j: j)])
        elif kind == 2:
            grp, hch, pst, lch = d // S5_GROUP_CH, S5_GROUP_CH, S5_STATE, S5_SCAN_CHUNK
            hf = norm_rows(x, g_mix, tm=tmh, slab=slab, out_dtype=F32, shift=sh_mix, scale=sc_mix, name="adaln_s5")
            w1, w2, coef = s5_weights(s5_lam_re[r], s5_lam_im[r], s5_b_re[r], s5_b_im[r],
                                      s5_c_re[r], s5_c_im[r], s5_log_dt[r])

            def to_chunks(hrows, nbatch, nseq):
                u = hrows.astype(BF16).reshape(nbatch, nseq // lch, lch, grp, hch)
                return u.transpose(3, 1, 0, 2, 4).reshape(grp, (nseq // lch) * nbatch, lch * hch)

            def from_chunks(y, nbatch, nseq):
                y = y.reshape(grp, nseq // lch, nbatch, lch, hch)
                return y.transpose(2, 1, 3, 0, 4).reshape(nbatch * nseq, d)

            h0_p = jnp.zeros((grp, 2, bp, 2 * pst), F32)
            h0_s = state_s5[:, r].transpose(2, 1, 0, 4, 3).reshape(grp, 2, bs, 2 * pst)
            gb = 2 if grp % 2 == 0 else 1
            y_p, fin_p = s5_scan(to_chunks(hf[:slab], bp, sp), w1, w2, coef, h0_p, gb=gb, nb=bp)
            y_s, _ = s5_scan(to_chunks(hf[slab:], bs, ns), w1, w2, coef, h0_s, gb=gb, nb=bs)
            outs['s5'] = fin_p.reshape(grp, 2, bp, 2, pst).transpose(2, 1, 0, 4, 3)[:, None]
            y = jnp.concatenate([from_chunks(y_p, bp, sp), from_chunks(y_s, bs, ns)], axis=0)
            wg = s5_w_glu[r].astype(BF16)
            bg = s5_b_glu[r][None]
            (x,) = fused_mm(
                name="s5_glu", m=t, k=d, n=d, tm=tmh, tn=tn, slab=slab,
                xs=[(y, 0), (hf, 0)], pro_consts=[s5_d[r][None]], prologue=_pro_s5_post,
                ws=[(wg, 0), (wg, d // tn)], epi_cols=[(bg, 0), (bg, d // tn)],
                epi_mods=[gt_mix], epi_rows=[x], epilogue=_epi_glu_residual,
                outs=[(d, F32, tn, lambda j: j)])
        else:
            w1c = conv_w_pw1[r].astype(BF16)
            b1c = conv_b_pw1[r][None]
            (z,) = fused_mm(
                name="conv_pw1_glu", m=t, k=d, n=d, tm=tmh, tn=tn, slab=slab,
                xs=[(x, 0)], pro_consts=[g_mix], pro_mods=[sh_mix, sc_mix], prologue=_pro_adaln,
                ws=[(w1c, 0), (w1c, d // tn)], epi_cols=[(b1c, 0), (b1c, d // tn)], epilogue=_epi_glu,
                outs=[(d, F32, tn, lambda j: j)])
            zc = dwconv(z, conv_w_dw[r], conv_b_dw[r][None], tile=sp, cb=tn, slab=slab, seq_prompt=sp)
            (x,) = fused_mm(
                name="conv_pw2", m=t, k=d, n=d, tm=tmh, tn=tn, slab=slab,
                xs=[(zc, 0)], pro_consts=[conv_ln_g[r][None], conv_ln_b[r][None]], prologue=_pro_ln_silu,
                ws=[(conv_w_pw2[r].astype(BF16), 0)], epi_cols=[(conv_b_pw2[r][None], 0)],
                epi_mods=[gt_mix], epi_rows=[x], epilogue=_epi_bias_residual,
                outs=[(d, F32, tn, lambda j: j)])
        x = moe_layer(x, norm_ffn_g[layer][None], sh_ffn, sc_ffn, gt_ffn, (w_r_hi, w_r_lo), moe_b_router,
                      moe_w1[layer], moe_w3[layer], moe_w2[layer], tm_rows=tmh, tm_moe=256, slab=slab)

    y = norm_rows(x, final_norm_g[None], tm=tmh, slab=slab, out_dtype=F32, name="final_norm")
    return (y[:slab].reshape(bp, sp, d), y[slab:].reshape(bs, ns, d),
            outs['ckv'], outs['krope'], outs['gqa_k'], outs['gqa_v'], outs['s5'])
```

```python
import functools
import math

import jax
import jax.numpy as jnp
from jax import lax
from jax.experimental import pallas as pl
from jax.experimental.pallas import tpu as pltpu

GRID_W = 64
NORM_EPS = 1e-6
ROPE_BASE = 10000.0
NEG_INF = -1e30

MLA_HEADS = 16
MLA_NOPE = 128
MLA_ROPE = 64
MLA_V = 128

GQA_HEADS = 32
GQA_KV_HEADS = 8
GQA_HEAD_DIM = 64
WINDOW = 128
BAND_BLOCK = 128

S5_GROUP_CH = 16
S5_STATE = 64
S5_SCAN_CHUNK = 16

CONV_WIDTH = 31
CONV_HALO = 16

N_EXPERT_GROUPS = 4
TOP_K = 2

LANES = 128
VMEM_LIMIT_BYTES = 56 * 1024 * 1024

BF16 = jnp.bfloat16
F32 = jnp.float32


def _cparams(sem):
    return pltpu.CompilerParams(dimension_semantics=sem, vmem_limit_bytes=VMEM_LIMIT_BYTES)


def _silu(x):
    return x * jax.nn.sigmoid(x)


def _gelu_tanh(x):
    return 0.5 * x * (1.0 + jnp.tanh(math.sqrt(2.0 / math.pi) * (x + 0.044715 * (x * x * x))))


def _rms(x, g):
    return (x * lax.rsqrt(jnp.mean(x * x, axis=-1, keepdims=True) + NORM_EPS)) * g


def _rope_rot(x, cos_t, sin_t):
    w = x.shape[-1]
    reps = w // LANES
    c = jnp.concatenate([cos_t] * reps, axis=1) if reps > 1 else cos_t
    s = jnp.concatenate([sin_t] * reps, axis=1) if reps > 1 else sin_t
    lane = lax.broadcasted_iota(jnp.int32, x.shape, 1)
    first_half = (lane % 64) < 32
    swapped = jnp.where(first_half, pltpu.roll(x, w - 32, 1), pltpu.roll(x, 32, 1))
    return x * c + swapped * s


def _fused_mm_body(*refs, n_x, n_pc, n_pm, n_w, n_ec, n_ek, n_em, n_er, n_et, n_out, prologue, epilogue):
    pos = 0

    def take(n):
        nonlocal pos
        r = refs[pos:pos + n]
        pos += n
        return r

    x_refs = take(n_x)
    pc_refs = take(n_pc)
    pm_refs = take(n_pm)
    w_refs = take(n_w)
    ec_refs = take(n_ec + n_ek)
    em_refs = take(n_em)
    er_refs = take(n_er)
    et_refs = take(n_et)
    out_refs = take(n_out)
    (h_ref,) = take(1)
    i = pl.program_id(0)
    j = pl.program_id(1)

    @pl.when(j == 0)
    def _():
        h = prologue([r[...] for r in x_refs], [r[...] for r in pc_refs], [r[0] for r in pm_refs])
        h_ref[...] = h.astype(h_ref.dtype)

    h = h_ref[...]
    accs = [jnp.dot(h, w_ref[...].astype(BF16), preferred_element_type=F32) for w_ref in w_refs]
    outs = epilogue(accs, i, j, [r[...] for r in ec_refs], [r[0] for r in em_refs],
                    [r[...] for r in er_refs], [r[...] for r in et_refs])
    for o_ref, o in zip(out_refs, outs):
        o_ref[...] = o.astype(o_ref.dtype)


def fused_mm(*, name, m, k, n, tm, tn, slab, xs, prologue, ws, epilogue, outs,
             pro_consts=(), pro_mods=(), epi_cols=(), epi_consts=(), epi_mods=(), epi_rows=(), epi_tabs=()):
    assert m % tm == 0 and n % tn == 0 and slab % tm == 0
    grid = (m // tm, n // tn)
    tiles_per_slab = slab // tm
    in_specs, args = [], []
    for arr, off in xs:
        in_specs.append(pl.BlockSpec((tm, k), functools.partial(lambda i, j, off: (i + off, 0), off=off)))
        args.append(arr)
    for arr in pro_consts:
        in_specs.append(pl.BlockSpec((1, k), lambda i, j: (0, 0)))
        args.append(arr)
    for arr in pro_mods:
        in_specs.append(pl.BlockSpec((1, 1, k), lambda i, j: (i // tiles_per_slab, 0, 0)))
        args.append(arr)
    for arr, off in ws:
        in_specs.append(pl.BlockSpec((k, tn), functools.partial(lambda i, j, off: (0, j + off), off=off)))
        args.append(arr)
    for arr, off in epi_cols:
        in_specs.append(pl.BlockSpec((1, tn), functools.partial(lambda i, j, off: (0, j + off), off=off)))
        args.append(arr)
    for arr in epi_consts:
        in_specs.append(pl.BlockSpec(arr.shape, lambda i, j: (0, 0)))
        args.append(arr)
    for arr in epi_mods:
        in_specs.append(pl.BlockSpec((1, 1, tn), lambda i, j: (i // tiles_per_slab, 0, j)))
        args.append(arr)
    for arr in epi_rows:
        in_specs.append(pl.BlockSpec((tm, tn), lambda i, j: (i, j)))
        args.append(arr)
    for arr in epi_tabs:
        in_specs.append(pl.BlockSpec((tm, arr.shape[1]), lambda i, j: (i % tiles_per_slab, 0)))
        args.append(arr)
    out_specs, out_shapes = [], []
    for cols, dtype, bcols, cfn in outs:
        out_specs.append(pl.BlockSpec((tm, bcols), functools.partial(lambda i, j, cfn: (i, cfn(j)), cfn=cfn)))
        out_shapes.append(jax.ShapeDtypeStruct((m, cols), dtype))
    body = functools.partial(
        _fused_mm_body, n_x=len(xs), n_pc=len(pro_consts), n_pm=len(pro_mods), n_w=len(ws),
        n_ec=len(epi_cols), n_ek=len(epi_consts), n_em=len(epi_mods), n_er=len(epi_rows), n_et=len(epi_tabs),
        n_out=len(outs), prologue=prologue, epilogue=epilogue)
    res = pl.pallas_call(
        body, grid=grid, in_specs=in_specs, out_specs=out_specs, out_shape=out_shapes,
        scratch_shapes=[pltpu.VMEM((tm, k), BF16)],
        compiler_params=_cparams(("parallel", "arbitrary")), name=name)(*args)
    return res


def _pro_cast(xv, cv, mv):
    return xv[0]


def _pro_adaln(xv, cv, mv):
    return _rms(xv[0], cv[0]) * (1.0 + mv[1]) + mv[0]


def _pro_rms(xv, cv, mv):
    return _rms(xv[0], cv[0])


def _pro_ln_silu(xv, cv, mv):
    x = xv[0]
    mu = jnp.mean(x, axis=-1, keepdims=True)
    xc = x - mu
    var = jnp.mean(xc * xc, axis=-1, keepdims=True)
    return _silu((xc * lax.rsqrt(var + NORM_EPS)) * cv[0] + cv[1])


def _pro_s5_post(xv, cv, mv):
    return _gelu_tanh(xv[0] + cv[0] * xv[1])


def _epi_plain(accs, i, j, ec, em, er, et):
    return (accs[0],)


def _epi_bias_residual(accs, i, j, ec, em, er, et):
    return (er[0] + em[0] * (accs[0] + ec[0]),)


def _epi_residual(accs, i, j, ec, em, er, et):
    return (er[0] + em[0] * accs[0],)


def _epi_glu(accs, i, j, ec, em, er, et):
    return ((accs[0] + ec[0]) * jax.nn.sigmoid(accs[1] + ec[1]),)


def _epi_glu_residual(accs, i, j, ec, em, er, et):
    return (er[0] + em[0] * ((accs[0] + ec[0]) * jax.nn.sigmoid(accs[1] + ec[1])),)


def _modulation_body(c_ref, w_ref, b_ref, o_ref):
    h = _silu(c_ref[...]).astype(BF16)
    o_ref[0] = jnp.dot(h, w_ref[0].astype(BF16), preferred_element_type=F32) + b_ref[0]


def modulation_all(cond, ada_w, ada_b, tn):
    depth, d, n = ada_w.shape
    rows = cond.shape[0]
    return pl.pallas_call(
        _modulation_body, grid=(depth, n // tn),
        in_specs=[pl.BlockSpec((rows, d), lambda l, j: (0, 0)),
                  pl.BlockSpec((1, d, tn), lambda l, j: (l, 0, j)),
                  pl.BlockSpec((1, 1, tn), lambda l, j: (l, 0, j))],
        out_specs=pl.BlockSpec((1, rows, tn), lambda l, j: (l, 0, j)),
        out_shape=jax.ShapeDtypeStruct((depth, rows, n), F32),
        compiler_params=_cparams(("parallel", "parallel")), name="modulation")(
            cond, ada_w, ada_b.reshape(depth, 1, n))


def _norm_body(*refs, modulated):
    x_ref, g_ref = refs[0], refs[1]
    y = _rms(x_ref[...], g_ref[...])
    if modulated:
        shift_ref, scale_ref = refs[2], refs[3]
        y = y * (1.0 + scale_ref[0]) + shift_ref[0]
    o_ref = refs[-1]
    o_ref[...] = y.astype(o_ref.dtype)


def norm_rows(x, g, *, tm, slab, out_dtype, shift=None, scale=None, name="norm"):
    m, d = x.shape
    tiles_per_slab = slab // tm
    in_specs = [pl.BlockSpec((tm, d), lambda i: (i, 0)), pl.BlockSpec((1, d), lambda i: (0, 0))]
    args = [x, g]
    modulated = shift is not None
    if modulated:
        in_specs += [pl.BlockSpec((1, 1, d), lambda i: (i // tiles_per_slab, 0, 0))] * 2
        args += [shift, scale]
    return pl.pallas_call(
        functools.partial(_norm_body, modulated=modulated),
        grid=(m // tm,), in_specs=in_specs, out_specs=pl.BlockSpec((tm, d), lambda i: (i, 0)),
        out_shape=jax.ShapeDtypeStruct((m, d), out_dtype),
        compiler_params=_cparams(("parallel",)), name=name)(*args)


def _route_body(x_ref, g_ref, shift_ref, scale_ref, whi_ref, wlo_ref, b_ref, tri_ref,
                h_ref, idx_ref, gate_ref, cnt_ref, carry_ref, *, n_groups):
    @pl.when(pl.program_id(0) == 0)
    def _():
        carry_ref[...] = jnp.zeros(carry_ref.shape, F32)

    y = _rms(x_ref[...], g_ref[...]) * (1.0 + scale_ref[0]) + shift_ref[0]
    y_hi = y.astype(BF16)
    h_ref[...] = y_hi
    y_lo = (y - y_hi.astype(F32)).astype(BF16)
    nt = (((1,), (1,)), ((), ()))
    w_hi = whi_ref[...]
    logits = (lax.dot_general(w_hi, y_hi, nt, preferred_element_type=F32)
              + lax.dot_general(w_hi, y_lo, nt, preferred_element_type=F32)
              + lax.dot_general(wlo_ref[...], y_hi, nt, preferred_element_type=F32))
    n_exp, tm = logits.shape
    epg = n_exp // n_groups
    scores = jax.nn.sigmoid(logits)
    biased = scores + b_ref[...]
    sub = lax.broadcasted_iota(jnp.int32, (epg, tm), 0).astype(F32)
    best = sel = loc1 = loc2 = None
    for grp in range(n_groups):
        blk = biased[grp * epg:(grp + 1) * epg]
        m1 = jnp.max(blk, axis=0, keepdims=True)
        i1 = jnp.min(jnp.where(blk == m1, sub, float(epg)), axis=0, keepdims=True)
        rest = jnp.where(sub == i1, -jnp.inf, blk)
        m2 = jnp.max(rest, axis=0, keepdims=True)
        i2 = jnp.min(jnp.where(rest == m2, sub, float(epg)), axis=0, keepdims=True)
        gsum = m1 + m2
        if grp == 0:
            best, sel, loc1, loc2 = gsum, jnp.zeros_like(gsum), i1, i2
        else:
            better = gsum > best
            best = jnp.where(better, gsum, best)
            sel = jnp.where(better, float(grp), sel)
            loc1 = jnp.where(better, i1, loc1)
            loc2 = jnp.where(better, i2, loc2)
    e1 = sel * epg + loc1
    e2 = sel * epg + loc2
    row = lax.broadcasted_iota(jnp.int32, (n_exp, tm), 0).astype(F32)
    hit1, hit2 = row == e1, row == e2
    s1 = jnp.sum(jnp.where(hit1, scores, 0.0), axis=0, keepdims=True)
    s2 = jnp.sum(jnp.where(hit2, scores, 0.0), axis=0, keepdims=True)
    total = s1 + s2
    onehot = jnp.where(hit1, 1.0, 0.0) + jnp.where(hit2, 1.0, 0.0)
    csum = jnp.dot(onehot.astype(BF16), tri_ref[...], preferred_element_type=F32)
    before = carry_ref[...] + csum - onehot
    r1 = jnp.sum(jnp.where(hit1, before, 0.0), axis=0, keepdims=True)
    r2 = jnp.sum(jnp.where(hit2, before, 0.0), axis=0, keepdims=True)
    carry = carry_ref[...] + csum[:, tm - 1:tm]
    carry_ref[...] = carry
    cnt_ref[...] = jnp.broadcast_to(carry, cnt_ref.shape)
    idx_ref[...] = jnp.concatenate([e1, e2, r1, r2, jnp.zeros((4, tm), F32)], axis=0).astype(jnp.int32)
    gate_ref[...] = jnp.concatenate([s1 / total, s2 / total, jnp.zeros((6, tm), F32)], axis=0)


def route(x, g, shift, scale, wt_hi, wt_lo, b_router, *, tm, slab):
    m, d = x.shape
    n_exp = wt_hi.shape[0]
    tiles_per_slab = slab // tm
    tri = jnp.triu(jnp.ones((tm, tm), BF16))
    const = lambda i: (0, 0)
    return pl.pallas_call(
        functools.partial(_route_body, n_groups=N_EXPERT_GROUPS),
        grid=(m // tm,),
        in_specs=[pl.BlockSpec((tm, d), lambda i: (i, 0)), pl.BlockSpec((1, d), const),
                  pl.BlockSpec((1, 1, d), lambda i: (i // tiles_per_slab, 0, 0)),
                  pl.BlockSpec((1, 1, d), lambda i: (i // tiles_per_slab, 0, 0)),
                  pl.BlockSpec((n_exp, d), const), pl.BlockSpec((n_exp, d), const),
                  pl.BlockSpec((n_exp, 1), const), pl.BlockSpec((tm, tm), const)],
        out_specs=[pl.BlockSpec((tm, d), lambda i: (i, 0)), pl.BlockSpec((8, tm), lambda i: (0, i)),
                   pl.BlockSpec((8, tm), lambda i: (0, i)), pl.BlockSpec((n_exp, LANES), const)],
        out_shape=[jax.ShapeDtypeStruct((m, d), BF16), jax.ShapeDtypeStruct((8, m), jnp.int32),
                   jax.ShapeDtypeStruct((8, m), F32), jax.ShapeDtypeStruct((n_exp, LANES), F32)],
        scratch_shapes=[pltpu.VMEM((n_exp, 1), F32)],
        compiler_params=_cparams(("arbitrary",)), name="adaln_route")(
            x, g, shift, scale, wt_hi, wt_lo, b_router.reshape(n_exp, 1), tri)


def _mla_attn_body(*refs, nk, nh, aliased):
    if aliased:
        refs = refs[1:]
    qn_ref, qr_ref, kn_ref, vt_ref, kr_ref, o_ref, m_ref, l_ref, acc_ref = refs
    ki = pl.program_id(3)

    @pl.when(ki == 0)
    def _():
        m_ref[...] = jnp.full(m_ref.shape, -jnp.inf, F32)
        l_ref[...] = jnp.zeros(l_ref.shape, F32)
        acc_ref[...] = jnp.zeros(acc_ref.shape, F32)

    kr = kr_ref[...]
    scores, probs = {}, {}

    def qk(hh):
        q = jnp.concatenate([qn_ref[:, hh * MLA_NOPE:(hh + 1) * MLA_NOPE],
                             qr_ref[:, hh * MLA_ROPE:(hh + 1) * MLA_ROPE]], axis=1)
        k = jnp.concatenate([kn_ref[:, hh * MLA_NOPE:(hh + 1) * MLA_NOPE], kr], axis=1)
        scores[hh] = lax.dot_general(k, q, (((1,), (1,)), ((), ())), preferred_element_type=F32)

    def softmax(hh):
        st = scores.pop(hh)
        m_prev = m_ref[hh]
        m_new = jnp.maximum(m_prev, jnp.max(st, axis=0, keepdims=True))
        alpha = jnp.exp2(m_prev - m_new)
        p = jnp.exp2(st - m_new)
        l_ref[hh] = alpha * l_ref[hh] + jnp.sum(p, axis=0, keepdims=True)
        m_ref[hh] = m_new
        probs[hh] = (alpha, p.astype(BF16))

    def pv(hh):
        alpha, p = probs.pop(hh)
        acc_ref[hh] = alpha * acc_ref[hh] + jnp.dot(
            vt_ref[hh * MLA_V:(hh + 1) * MLA_V, :], p, preferred_element_type=F32)

    for t in range(nh + 2):
        if t < nh:
            qk(t)
        if 0 <= t - 1 < nh:
            softmax(t - 1)
        if 0 <= t - 2 < nh:
            pv(t - 2)

    @pl.when(ki == nk - 1)
    def _():
        o_ref[...] = jnp.concatenate([(acc_ref[hh] / l_ref[hh]).T for hh in range(nh)],
                                     axis=1).astype(o_ref.dtype)


def mla_attention(qn, qr, kn, vt, kr, *, n_heads, batch, nq_rows, nk_rows, q_row0, k_row0, tq, tk, prev_out=None):
    t = qn.shape[0]
    nh = min(4, n_heads)
    hg = n_heads // nh
    nq, nk = nq_rows // tq, nk_rows // tk
    assert q_row0 % tq == 0 and k_row0 % tk == 0 and nq_rows % tq == 0 and nk_rows % tk == 0
    qb0, kb0 = q_row0 // tq, k_row0 // tk

    def qmap(b, h, qi, ki):
        return (qb0 + b * nq + qi, h)

    def kmap(b, h, qi, ki):
        return (kb0 + b * nk + ki, h)

    in_specs = [pl.BlockSpec((tq, nh * MLA_NOPE), qmap),
                pl.BlockSpec((tq, nh * MLA_ROPE), qmap),
                pl.BlockSpec((tk, nh * MLA_NOPE), kmap),
                pl.BlockSpec((nh * MLA_V, tk), lambda b, h, qi, ki: (h, kb0 + b * nk + ki)),
                pl.BlockSpec((tk, MLA_ROPE), lambda b, h, qi, ki: (kb0 + b * nk + ki, 0))]
    args = [qn, qr, kn, vt, kr]
    aliases = {}
    if prev_out is not None:
        in_specs = [pl.BlockSpec(memory_space=pl.ANY)] + in_specs
        args = [prev_out] + args
        aliases = {0: 0}
    return pl.pallas_call(
        functools.partial(_mla_attn_body, nk=nk, nh=nh, aliased=prev_out is not None),
        grid=(batch, hg, nq, nk), in_specs=in_specs,
        out_specs=pl.BlockSpec((tq, nh * MLA_V), qmap),
        out_shape=jax.ShapeDtypeStruct((t, n_heads * MLA_V), BF16),
        scratch_shapes=[pltpu.VMEM((nh, 1, tq), F32), pltpu.VMEM((nh, 1, tq), F32),
                        pltpu.VMEM((nh, MLA_V, tq), F32)],
        input_output_aliases=aliases,
        compiler_params=_cparams(("parallel", "parallel", "parallel", "arbitrary")),
        name="mla_attention")(*args)


def _mm_nt_body(w_ref, x_ref, o_ref):
    o_ref[...] = lax.dot_general(w_ref[...], x_ref[...], (((1,), (1,)), ((), ())),
                                 preferred_element_type=F32).astype(o_ref.dtype)


def mm_nt(wt, x, *, tm, name):
    n, k = wt.shape
    m = x.shape[0]
    return pl.pallas_call(
        _mm_nt_body, grid=(m // tm,),
        in_specs=[pl.BlockSpec((n, k), lambda i: (0, 0)), pl.BlockSpec((tm, k), lambda i: (i, 0))],
        out_specs=pl.BlockSpec((n, tm), lambda i: (0, i)),
        out_shape=jax.ShapeDtypeStruct((n, m), BF16),
        compiler_params=_cparams(("parallel",)), name=name)(wt, x)


def _gqa_attn_body(*refs, n_seg, window, nq, aliased, group):
    if aliased:
        refs = refs[1:]
    sink_ref, q_ref = refs[0], refs[1]
    k_refs = refs[2:2 + n_seg]
    v_refs = refs[2 + n_seg:2 + 2 * n_seg]
    o_ref = refs[2 + 2 * n_seg]
    p_idx = pl.program_id(1)
    qi = pl.program_id(2)
    d = GQA_HEAD_DIM
    q = q_ref[...]
    tq = q.shape[0]
    ks = [r[...] for r in k_refs]
    vs = [r[...] for r in v_refs]
    if ks[-1].ndim == 3:
        ks[-1] = ks[-1][0]
        vs[-1] = vs[-1][0]
    n_loc = 3 * BAND_BLOCK
    if window:
        key = lax.broadcasted_iota(jnp.int32, (n_loc, tq), 0)
        qry = lax.broadcasted_iota(jnp.int32, (n_loc, tq), 1)
        lo = jnp.where(qi > 0, 0, BAND_BLOCK)
        hi = jnp.where(qi < nq - 1, n_loc, 2 * BAND_BLOCK)
        valid1 = (jnp.abs(qry + BAND_BLOCK - key) <= WINDOW) & (key >= lo) & (key < hi)
        valid = jnp.concatenate([valid1.astype(F32)] * group, axis=1) > 0.5
    log2e = math.log2(math.e)
    o_t = []
    for kvh in range(2):
        k = jnp.concatenate([kk[:, kvh * d:(kvh + 1) * d] for kk in ks], axis=0)
        v = jnp.concatenate([vv[:, kvh * d:(kvh + 1) * d] for vv in vs], axis=0)
        base = kvh * group * d
        q4 = jnp.concatenate([q[:, base + g * d: base + (g + 1) * d] for g in range(group)], axis=0)
        s = lax.dot_general(k, q4, (((1,), (1,)), ((), ())), preferred_element_type=F32)
        head0 = (p_idx * 2 + kvh) * group
        snk = jnp.concatenate([jnp.full((1, tq), sink_ref[head0 + g] * log2e, F32) for g in range(group)], axis=1)
        if window:
            s_loc = jnp.where(valid, s[:n_loc], NEG_INF)
            s_ctx = s[n_loc:]
            m = jnp.maximum(jnp.maximum(jnp.max(s_loc, axis=0, keepdims=True),
                                        jnp.max(s_ctx, axis=0, keepdims=True)), snk)
            p = jnp.concatenate([jnp.exp2(s_loc - m), jnp.exp2(s_ctx - m)], axis=0)
        else:
            m = jnp.maximum(jnp.max(s, axis=0, keepdims=True), snk)
            p = jnp.exp2(s - m)
        denom = jnp.sum(p, axis=0, keepdims=True) + jnp.exp2(snk - m)
        o_t.append(lax.dot_general(v, p.astype(BF16), (((0,), (0,)), ((), ())),
                                   preferred_element_type=F32) / denom)
    o = jnp.concatenate(o_t, axis=0).T
    pieces = [o[g * tq:(g + 1) * tq, kvh * d:(kvh + 1) * d] for kvh in range(2) for g in range(group)]
    o_ref[...] = jnp.concatenate(pieces, axis=1).astype(o_ref.dtype)


def gqa_attention(qkv, sink, *, batch, n_rows, q_row0, window, ctx_k=None, ctx_v=None, prev_out=None):
    t = qkv.shape[0]
    d, h, kvh = GQA_HEAD_DIM, GQA_HEADS, GQA_KV_HEADS
    group = h // kvh
    pairs = kvh // 2
    qw = 2 * group * d
    kcol0 = (h * d) // LANES
    vcol0 = (h * d + kvh * d) // LANES
    tq = BAND_BLOCK if window else n_rows
    nq = n_rows // tq
    qb0 = q_row0 // tq
    assert q_row0 % tq == 0

    def qmap(b, p, i):
        return (qb0 + b * nq + i, p)

    in_specs = [pl.BlockSpec(memory_space=pltpu.SMEM), pl.BlockSpec((tq, qw), qmap)]
    args = [sink, qkv]
    if window:
        def kv_specs(col0):
            return [
                pl.BlockSpec((tq, LANES), lambda b, p, i: (qb0 + b * nq + jnp.maximum(i - 1, 0), col0 + p)),
                pl.BlockSpec((tq, LANES), lambda b, p, i: (qb0 + b * nq + i, col0 + p)),
                pl.BlockSpec((tq, LANES), lambda b, p, i: (qb0 + b * nq + jnp.minimum(i + 1, nq - 1), col0 + p)),
            ]
        n_ctx = ctx_k.shape[1]
        ctx_spec = pl.BlockSpec((1, n_ctx, LANES), lambda b, p, i: (b, 0, p))
        in_specs += kv_specs(kcol0) + [ctx_spec] + kv_specs(vcol0) + [ctx_spec]
        args += [qkv, qkv, qkv, ctx_k, qkv, qkv, qkv, ctx_v]
        n_seg = 4
    else:
        in_specs += [pl.BlockSpec((tq, LANES), lambda b, p, i: (qb0 + b * nq + i, kcol0 + p)),
                     pl.BlockSpec((tq, LANES), lambda b, p, i: (qb0 + b * nq + i, vcol0 + p))]
        args += [qkv, qkv]
        n_seg = 1
    aliases = {}
    if prev_out is not None:
        in_specs = [pl.BlockSpec(memory_space=pl.ANY)] + in_specs
        args = [prev_out] + args
        aliases = {0: 0}
    return pl.pallas_call(
        functools.partial(_gqa_attn_body, n_seg=n_seg, window=window, nq=nq,
                          aliased=prev_out is not None, group=group),
        grid=(batch, pairs, nq), in_specs=in_specs,
        out_specs=pl.BlockSpec((tq, qw), qmap),
        out_shape=jax.ShapeDtypeStruct((t, h * d), BF16),
        input_output_aliases=aliases,
        compiler_params=_cparams(("parallel", "parallel", "parallel")),
        name="gqa_window_attention" if window else "gqa_dense_attention")(*args)


def _s5_body(u_ref, w1_ref, w2_ref, coef_ref, h0_ref, y_ref, fin_ref, z_ref, sp_ref, *, gb, nb, n_steps):
    lc = S5_SCAN_CHUNK * S5_GROUP_CH
    ns = 2 * S5_STATE
    for g in range(gb):
        z_ref[g] = jnp.dot(u_ref[g], w1_ref[g], preferred_element_type=F32)

    coef = coef_ref[...]

    def step(c, carry):
        new = []
        rf = pl.multiple_of(c * nb, nb)
        rb = pl.multiple_of((n_steps - 1 - c) * nb, nb)
        for g in range(gb):
            for dirn, rows in ((0, rf), (1, rb)):
                s, tw = carry[4 * g + 2 * dirn], carry[4 * g + 2 * dirn + 1]
                a1 = coef[g, 2 * dirn:2 * dirn + 1, :]
                a2 = coef[g, 2 * dirn + 1:2 * dirn + 2, :]
                sp_ref[g, pl.ds(rows, nb), dirn * ns:(dirn + 1) * ns] = s
                c0 = 2 * dirn * ns
                new.append(a1 * s + a2 * tw + z_ref[g, pl.ds(rows, nb), c0:c0 + ns])
                new.append(a1 * tw - a2 * s + z_ref[g, pl.ds(rows, nb), c0 + ns:c0 + 2 * ns])
        return tuple(new)

    init = []
    for g in range(gb):
        for dirn in range(2):
            s0 = h0_ref[g, dirn]
            init += [s0, pltpu.roll(s0, S5_STATE, 1)]
    fin = lax.fori_loop(0, n_steps, step, tuple(init))
    for g in range(gb):
        fin_ref[g, 0] = fin[4 * g]
        fin_ref[g, 1] = fin[4 * g + 2]
        y_ref[g] = (jnp.dot(u_ref[g], w2_ref[g, 0:lc, :], preferred_element_type=F32)
                    + jnp.dot(sp_ref[g].astype(BF16), w2_ref[g, lc:, :], preferred_element_type=F32))


def s5_scan(u, w1, w2, coef, h0, *, gb, nb):
    g, cols, lc = u.shape
    n_steps = cols // nb
    ns2 = 4 * S5_STATE
    return pl.pallas_call(
        functools.partial(_s5_body, gb=gb, nb=nb, n_steps=n_steps),
        grid=(g // gb,),
        in_specs=[pl.BlockSpec((gb, cols, lc), lambda i: (i, 0, 0)),
                  pl.BlockSpec((gb, lc, 2 * ns2), lambda i: (i, 0, 0)),
                  pl.BlockSpec((gb, lc + ns2, lc), lambda i: (i, 0, 0)),
                  pl.BlockSpec((gb, 4, 2 * S5_STATE), lambda i: (i, 0, 0)),
                  pl.BlockSpec((gb, 2, nb, 2 * S5_STATE), lambda i: (i, 0, 0, 0))],
        out_specs=[pl.BlockSpec((gb, cols, lc), lambda i: (i, 0, 0)),
                   pl.BlockSpec((gb, 2, nb, 2 * S5_STATE), lambda i: (i, 0, 0, 0))],
        out_shape=[jax.ShapeDtypeStruct((g, cols, lc), F32),
                   jax.ShapeDtypeStruct((g, 2, nb, 2 * S5_STATE), F32)],
        scratch_shapes=[pltpu.VMEM((gb, cols, 2 * ns2), F32), pltpu.VMEM((gb, cols, ns2), F32)],
        compiler_params=_cparams(("parallel",)), name="s5_scan")(u, w1, w2, coef, h0)


def _s5_pack_body(x_ref, sel_ref, o_ref, *, n_tok, gpt, nbb):
    rows = o_ref.shape[1]
    w = o_ref.shape[2] // nbb
    acc = None
    for s in range(n_tok):
        xs = x_ref[pl.ds(s, nbb * rows, stride=n_tok), :].astype(BF16)
        part = jnp.dot(xs, sel_ref[s], preferred_element_type=F32)
        acc = part if acc is None else acc + part
    for g in range(gpt):
        for bl in range(nbb):
            o_ref[g, :, bl * w:(bl + 1) * w] = acc[bl * rows:(bl + 1) * rows, g * w:(g + 1) * w].astype(o_ref.dtype)


def _s5_unpack_body(*refs, n_tok, gpt, nbb, aliased):
    if aliased:
        refs = refs[1:]
    y_ref, q_ref, o_ref = refs
    rows = y_ref.shape[1]
    w = y_ref.shape[2] // nbb
    ycat = jnp.concatenate(
        [jnp.concatenate([y_ref[g, :, bl * w:(bl + 1) * w] for g in range(gpt)], axis=1) for bl in range(nbb)],
        axis=0)
    y_hi = ycat.astype(BF16)
    y_lo = (ycat - y_hi.astype(F32)).astype(BF16)
    for j in range(n_tok):
        qj = q_ref[j]
        o_ref[pl.ds(j, nbb * rows, stride=n_tok), :] = (jnp.dot(y_hi, qj, preferred_element_type=F32)
                                                        + jnp.dot(y_lo, qj, preferred_element_type=F32))


def _s5_selectors(n_tok, hch):
    gpt = LANES // hch
    lane = jnp.arange(LANES)
    col = (lane // hch) * (n_tok * hch) + (lane % hch)
    sel = jax.nn.one_hot(col[None, :] + hch * jnp.arange(n_tok)[:, None], gpt * n_tok * hch, dtype=BF16)
    return sel, sel.transpose(0, 2, 1)


def s5_pack(hf, sel, *, row0, nbatch, nseq, rows_blk, seqs_blk=1):
    t, d = hf.shape
    n_tok, hch = S5_SCAN_CHUNK, S5_GROUP_CH
    gpt = LANES // hch
    chunks = nseq // n_tok
    cb = chunks // rows_blk
    blk_rows = seqs_blk * rows_blk * n_tok
    assert row0 % blk_rows == 0 and chunks % rows_blk == 0 and nbatch % seqs_blk == 0
    assert seqs_blk == 1 or cb == 1
    rb0 = row0 // blk_rows
    return pl.pallas_call(
        functools.partial(_s5_pack_body, n_tok=n_tok, gpt=gpt, nbb=seqs_blk),
        grid=(nbatch // seqs_blk, cb, d // LANES),
        in_specs=[pl.BlockSpec((blk_rows, LANES), lambda b, c, l: (rb0 + b * cb + c, l)),
                  pl.BlockSpec(sel.shape, lambda b, c, l: (0, 0, 0))],
        out_specs=pl.BlockSpec((gpt, rows_blk, seqs_blk * n_tok * hch), lambda b, c, l: (l, c, b)),
        out_shape=jax.ShapeDtypeStruct((d // hch, chunks, nbatch * n_tok * hch), BF16),
        compiler_params=_cparams(("parallel", "parallel", "parallel")), name="s5_pack")(
            hf, sel).reshape(d // hch, chunks, nbatch, n_tok * hch)


def s5_unpack(y4, unsel, *, t, row0, rows_blk, seqs_blk=1, prev_out=None):
    grp, chunks, nbatch, w = y4.shape
    n_tok, hch = S5_SCAN_CHUNK, S5_GROUP_CH
    gpt = LANES // hch
    d = grp * hch
    cb = chunks // rows_blk
    blk_rows = seqs_blk * rows_blk * n_tok
    assert seqs_blk == 1 or cb == 1
    rb0 = row0 // blk_rows
    in_specs = [pl.BlockSpec((gpt, rows_blk, seqs_blk * w), lambda b, c, l: (l, c, b)),
                pl.BlockSpec(unsel.shape, lambda b, c, l: (0, 0, 0))]
    args = [y4.reshape(grp, chunks, nbatch * w), unsel]
    aliases = {}
    if prev_out is not None:
        in_specs = [pl.BlockSpec(memory_space=pl.ANY)] + in_specs
        args = [prev_out] + args
        aliases = {0: 0}
    return pl.pallas_call(
        functools.partial(_s5_unpack_body, n_tok=n_tok, gpt=gpt, nbb=seqs_blk, aliased=prev_out is not None),
        grid=(nbatch // seqs_blk, cb, d // LANES), in_specs=in_specs,
        out_specs=pl.BlockSpec((blk_rows, LANES), lambda b, c, l: (rb0 + b * cb + c, l)),
        out_shape=jax.ShapeDtypeStruct((t, d), F32), input_output_aliases=aliases,
        compiler_params=_cparams(("parallel", "parallel", "parallel")), name="s5_unpack")(*args)


def s5_weights(lam_re, lam_im, b_re, b_im, c_re, c_im, log_dt):
    hp = lax.Precision.HIGHEST
    L, H, P = S5_SCAN_CHUNK, S5_GROUP_CH, S5_STATE
    G = lam_re.shape[1]
    dt = jnp.exp(log_dt)[..., None]
    lr, li = lam_re, lam_im
    mag = jnp.exp(lr * dt)
    ar, ai = mag * jnp.cos(li * dt), mag * jnp.sin(li * dt)
    den = lr * lr + li * li
    cr = ((ar - 1.0) * lr + ai * li) / den
    ci = (ai * lr - (ar - 1.0) * li) / den
    bbr = cr[..., None] * b_re - ci[..., None] * b_im
    bbi = cr[..., None] * b_im + ci[..., None] * b_re
    kk = jnp.arange(L + 1, dtype=F32)[:, None, None, None]
    pmag = jnp.exp(kk * (lr * dt)[None])
    pr, pi = pmag * jnp.cos(kk * (li * dt)[None]), pmag * jnp.sin(kk * (li * dt)[None])
    car = c_re[None] * pr[:, :, :, None, :] - c_im[None] * pi[:, :, :, None, :]
    cai = c_re[None] * pi[:, :, :, None, :] + c_im[None] * pr[:, :, :, None, :]
    mker = (jnp.einsum('ldgop,dgpi->ldgoi', car[:L], bbr, precision=hp)
            - jnp.einsum('ldgop,dgpi->ldgoi', cai[:L], bbi, precision=hp))
    s_idx = jnp.arange(L)[:, None]
    j_idx = jnp.arange(L)[None, :]

    def toeplitz(m, lag, mask):
        t = m[jnp.clip(lag, 0, L - 1)] * mask[:, :, None, None, None]
        return t.transpose(2, 0, 4, 1, 3)

    t_all = (toeplitz(mker[:, 0], j_idx - s_idx, (j_idx >= s_idx).astype(F32))
             + toeplitz(mker[:, 1], s_idx - j_idx, (s_idx >= j_idx).astype(F32))).reshape(G, L * H, L * H)

    def bc(dirn, powers):
        qr_, qi_ = pr[powers, dirn], pi[powers, dirn]
        re = qr_[..., None] * bbr[dirn][None] - qi_[..., None] * bbi[dirn][None]
        im = qr_[..., None] * bbi[dirn][None] + qi_[..., None] * bbr[dirn][None]
        return jnp.concatenate([re, im], axis=2).transpose(1, 0, 3, 2).reshape(G, L * H, 2 * P)

    def cc(dirn, powers):
        wr = car[powers, dirn]
        wi = cai[powers, dirn]
        return jnp.concatenate([wr, -wi], axis=3).transpose(1, 3, 0, 2).reshape(G, 2 * P, L * H)

    ar_l = jnp.arange(L)
    def with_twin(m):
        return jnp.concatenate([m, m[..., P:], m[..., :P]], axis=2)

    w1 = jnp.concatenate([with_twin(bc(0, L - 1 - ar_l)), with_twin(bc(1, ar_l))], axis=2)
    w2 = jnp.concatenate([t_all, cc(0, ar_l + 1), cc(1, L - ar_l)], axis=1)
    coef = jnp.stack([jnp.concatenate([pr[L, 0], pr[L, 0]], -1), jnp.concatenate([-pi[L, 0], pi[L, 0]], -1),
                      jnp.concatenate([pr[L, 1], pr[L, 1]], -1), jnp.concatenate([-pi[L, 1], pi[L, 1]], -1)], axis=1)
    return w1.astype(BF16), w2.astype(BF16), coef


def _dwconv_body(cur_ref, prev_ref, next_ref, w_ref, b_ref, o_ref, zp_ref, *, tile, slab, seq_prompt, rows_sub):
    t = pl.program_id(0)
    row0 = t * tile
    seq_len = jnp.where(row0 < slab, seq_prompt, slab)
    pos = row0 % seq_len
    has_prev = pos != 0
    has_next = pos + tile != seq_len
    halo = CONV_HALO
    sub = 8
    rows = tile + 2 * halo
    zp_ref[0, 0:halo, :] = jnp.where(has_prev, prev_ref[...], 0.0)
    zp_ref[0, halo:halo + tile, :] = cur_ref[...]
    zp_ref[0, halo + tile:, :] = jnp.where(has_next, next_ref[...], 0.0)
    for j in range(1, sub):
        zp_ref[j, 0:rows - sub, :] = zp_ref[0, j:j + rows - sub, :]
    w = w_ref[...]
    off = halo - CONV_WIDTH // 2
    for r0 in range(0, tile, rows_sub):
        acc = jnp.zeros((rows_sub, w.shape[1]), F32) + b_ref[...]
        for kk in range(CONV_WIDTH):
            q, j = divmod(off + kk, sub)
            acc = acc + w[kk:kk + 1, :] * zp_ref[j, r0 + sub * q:r0 + sub * q + rows_sub, :]
        o_ref[r0:r0 + rows_sub, :] = acc


def dwconv(z, w, b, *, tile, cb, slab, seq_prompt, rows_sub=32):
    m, d = z.shape
    halo = CONV_HALO
    hb = tile // halo
    n_halo_blocks = m // halo
    assert seq_prompt % tile == 0 and slab % tile == 0 and tile % halo == 0
    return pl.pallas_call(
        functools.partial(_dwconv_body, tile=tile, slab=slab, seq_prompt=seq_prompt, rows_sub=rows_sub),
        grid=(m // tile, d // cb),
        in_specs=[pl.BlockSpec((tile, cb), lambda t, c: (t, c)),
                  pl.BlockSpec((halo, cb), lambda t, c: (jnp.maximum(t * hb - 1, 0), c)),
                  pl.BlockSpec((halo, cb), lambda t, c: (jnp.minimum((t + 1) * hb, n_halo_blocks - 1), c)),
                  pl.BlockSpec((CONV_WIDTH, cb), lambda t, c: (0, c)),
                  pl.BlockSpec((1, cb), lambda t, c: (0, c))],
        out_specs=pl.BlockSpec((tile, cb), lambda t, c: (t, c)),
        out_shape=jax.ShapeDtypeStruct((m, d), F32),
        scratch_shapes=[pltpu.VMEM((8, tile + 2 * halo, cb), F32)],
        compiler_params=_cparams(("parallel", "parallel")), name="dwconv")(z, z, z, w, b)


def _moe_body(be_ref, nu_ref, x_ref, w1_ref, w3_ref, w2_ref, o_ref, w1s, w3s, w2s):
    b = pl.program_id(0)
    prev = be_ref[jnp.maximum(b - 1, 0)]

    @pl.when((b == 0) | (be_ref[b] != prev))
    def _():
        w1s[...] = w1_ref[0].astype(BF16)
        w3s[...] = w3_ref[0].astype(BF16)
        w2s[...] = w2_ref[0].astype(BF16)

    @pl.when(b < nu_ref[0])
    def _():
        x = x_ref[...]
        a = jnp.dot(x, w1s[...], preferred_element_type=F32)
        g = jnp.dot(x, w3s[...], preferred_element_type=F32)
        o_ref[...] = jnp.dot((_silu(a) * g).astype(BF16), w2s[...], preferred_element_type=F32)

    @pl.when(b >= nu_ref[0])
    def _():
        o_ref[...] = jnp.zeros(o_ref.shape, o_ref.dtype)


def moe_experts(xg, block_expert, n_used, w1, w3, w2, *, tm):
    mp, d = xg.shape
    e, _, f = w1.shape
    grid_spec = pltpu.PrefetchScalarGridSpec(
        num_scalar_prefetch=2, grid=(mp // tm,),
        in_specs=[pl.BlockSpec((tm, d), lambda b, be, nu: (b, 0)),
                  pl.BlockSpec((1, d, f), lambda b, be, nu: (be[b], 0, 0)),
                  pl.BlockSpec((1, d, f), lambda b, be, nu: (be[b], 0, 0)),
                  pl.BlockSpec((1, f, d), lambda b, be, nu: (be[b], 0, 0))],
        out_specs=pl.BlockSpec((tm, d), lambda b, be, nu: (b, 0)),
        scratch_shapes=[pltpu.VMEM((d, f), BF16), pltpu.VMEM((d, f), BF16), pltpu.VMEM((f, d), BF16)])
    return pl.pallas_call(
        _moe_body, grid_spec=grid_spec, out_shape=jax.ShapeDtypeStruct((mp, d), F32),
        compiler_params=_cparams(("arbitrary",)), name="moe_experts")(block_expert, n_used, xg, w1, w3, w2)


def _combine_body(x_ref, y1_ref, y2_ref, g_ref, gm_ref, o_ref):
    g = g_ref[...]
    o_ref[...] = x_ref[...] + gm_ref[0] * (g[:, 0:1] * y1_ref[...] + g[:, 1:2] * y2_ref[...])


def moe_combine(x, y1, y2, gates, gate_mod, *, tm, slab):
    m, d = x.shape
    tiles_per_slab = slab // tm
    row = pl.BlockSpec((tm, d), lambda i: (i, 0))
    return pl.pallas_call(
        _combine_body, grid=(m // tm,),
        in_specs=[row, row, row, pl.BlockSpec((tm, LANES), lambda i: (i, 0)),
                  pl.BlockSpec((1, 1, d), lambda i: (i // tiles_per_slab, 0, 0))],
        out_specs=row, out_shape=jax.ShapeDtypeStruct((m, d), F32),
        compiler_params=_cparams(("parallel",)), name="moe_combine")(x, y1, y2, gates, gate_mod)


def moe_layer(x, norm_g, shift, scale, gate_mod, router_w, b_router, w1, w3, w2, *, tm_rows, tm_moe, slab):
    t, d = x.shape
    n_exp = w1.shape[0]
    hb, idx, gate8, cnt = route(x, norm_g, shift, scale, router_w[0], router_w[1], b_router, tm=tm_rows, slab=slab)
    expert = idx[0:TOP_K].T
    rank = idx[TOP_K:2 * TOP_K].T
    gate = gate8[0:TOP_K].T
    counts = cnt[:, 0].astype(jnp.int32)
    padded = (counts + tm_moe - 1) // tm_moe * tm_moe
    pends = jnp.cumsum(padded)
    pstarts = pends - padded
    dest = (pstarts[expert] + rank).astype(jnp.int32).reshape(-1)
    n_assign = t * TOP_K
    n_blocks = (n_assign + n_exp * (tm_moe - 1) + tm_moe - 1) // tm_moe
    block_row0 = jnp.arange(n_blocks, dtype=jnp.int32) * tm_moe
    block_expert = jnp.minimum(jnp.sum((pends[None, :] <= block_row0[:, None]).astype(jnp.int32), axis=1),
                               n_exp - 1).astype(jnp.int32)
    n_used = (pends[-1] // tm_moe).astype(jnp.int32).reshape(1)
    tok = jnp.arange(n_assign, dtype=jnp.int32) // TOP_K
    src = jnp.zeros((n_blocks * tm_moe,), jnp.int32).at[dest].set(tok)
    xg = hb.at[src].get(mode="promise_in_bounds")
    yg = moe_experts(xg, block_expert, n_used, w1, w3, w2, tm=tm_moe)
    dest2 = dest.reshape(t, TOP_K)
    y1 = yg.at[dest2[:, 0]].get(mode="promise_in_bounds")
    y2 = yg.at[dest2[:, 1]].get(mode="promise_in_bounds")
    gates = jnp.pad(gate, ((0, 0), (0, LANES - TOP_K)))
    return moe_combine(x, y1, y2, gates, gate_mod, tm=tm_rows, slab=slab)


def _rope_tables(n_tokens):
    rows = n_tokens // GRID_W
    r = jnp.repeat(jnp.arange(rows), GRID_W).astype(F32)
    col = jnp.tile(jnp.arange(GRID_W), rows).astype(F32)
    n_freq = 64 // 4
    inv = ROPE_BASE ** (-jnp.arange(n_freq, dtype=F32) / n_freq)
    ang = jnp.concatenate([r[:, None] * inv, col[:, None] * inv], axis=-1)
    c, s = jnp.cos(ang), jnp.sin(ang)
    return jnp.concatenate([c, c, c, c], axis=1), jnp.concatenate([-s, s, -s, s], axis=1)


def _pick_tile(n, pref):
    t = pref
    while n % t:
        t //= 2
    return t


def kernel(x_prompt, x_sample, cache_mla_ckv, cache_mla_krope, cache_gqa_k, cache_gqa_v, state_s5, c, c_ctx, norm_mix_g, norm_ffn_g, ada_w, ada_b, final_norm_g, mla_wq_a, mla_q_norm, mla_wq_b, mla_wkv_a, mla_kv_norm, mla_wkv_b, mla_wo, gqa_wq, gqa_wk, gqa_wv, gqa_wo, gqa_sink, s5_lam_re, s5_lam_im, s5_b_re, s5_b_im, s5_c_re, s5_c_im, s5_log_dt, s5_d, s5_w_glu, s5_b_glu, conv_w_pw1, conv_b_pw1, conv_w_dw, conv_b_dw, conv_ln_g, conv_ln_b, conv_w_pw2, conv_b_pw2, moe_w_router, moe_b_router, moe_w1, moe_w3, moe_w2):
    bp, sp, d = x_prompt.shape
    bs, ns, _ = x_sample.shape
    depth = ada_w.shape[0]
    past = cache_mla_ckv.shape[2]
    slab = ns
    assert bp * sp == slab, "prompt rows must fill exactly one slab"
    n_slab = 1 + bs
    t = n_slab * slab
    tm = _pick_tile(slab, 1024)
    tmh = _pick_tile(slab, 512)
    tn = _pick_tile(d, 512)
    x = jnp.concatenate([x_prompt.reshape(slab, d), x_sample.reshape(bs * slab, d)], axis=0)

    rows_pad = -(-n_slab // 16) * 16
    cond = jnp.concatenate([c_ctx[None], c, jnp.zeros((rows_pad - n_slab, d), F32)], axis=0)
    mod = modulation_all(cond, ada_w, ada_b, _pick_tile(6 * d, 1024))
    mod = mod[:, :n_slab].reshape(depth, n_slab, 6, 1, d).transpose(0, 2, 1, 3, 4)

    cos_t, sin_t = _rope_tables(ns)
    w_r = moe_w_router.T
    w_r_hi = w_r.astype(BF16)
    w_r_lo = (w_r - w_r_hi.astype(F32)).astype(BF16)
    n_mixers = 4
    outs = {}

    for layer in range(depth):
        kind, r = layer % n_mixers, layer // n_mixers
        sh_mix, sc_mix, gt_mix, sh_ffn, sc_ffn, gt_ffn = [mod[layer, q] for q in range(6)]
        g_mix = norm_mix_g[layer][None]
        if kind == 0:
            qrank = mla_wq_a.shape[2]
            kvrank = mla_kv_norm.shape[1]
            hds = MLA_HEADS
            wa = jnp.concatenate([mla_wq_a[r], mla_wkv_a[r],
                                  jnp.zeros((d, LANES - MLA_ROPE), F32)], axis=1).astype(BF16)
            na = wa.shape[1]
            q_norm, kv_norm = mla_q_norm[r][None], mla_kv_norm[r][None]

            def epi_a(accs, i, j, ec, em, er, et, qrank=qrank, kvrank=kvrank, tiles=slab // tmh):
                a = accs[0]
                qn = _rms(a[:, :qrank], ec[0])
                ckv = _rms(a[:, qrank:qrank + kvrank], ec[1])
                kr = a[:, qrank + kvrank:]
                kr = jnp.where(i >= tiles, _rope_rot(kr, et[0], et[1]), kr)
                return qn, ckv, kr

            qn, ckv, kr = fused_mm(
                name="mla_proj_a", m=t, k=d, n=na, tm=tmh, tn=na, slab=slab,
                xs=[(x, 0)], pro_consts=[g_mix], pro_mods=[sh_mix, sc_mix], prologue=_pro_adaln,
                ws=[(wa, 0)], epi_tabs=[cos_t, sin_t], epi_consts=[q_norm, kv_norm], epilogue=epi_a,
                outs=[(qrank, BF16, qrank, lambda j: 0), (kvrank, F32, kvrank, lambda j: 0),
                      (LANES, F32, LANES, lambda j: 0)])
            outs['ckv'] = ckv[:slab].reshape(bp, 1, sp, kvrank)
            outs['krope'] = kr[:slab, :MLA_ROPE].reshape(bp, 1, sp, MLA_ROPE)
            wqb = mla_wq_b[r].reshape(qrank, hds, MLA_NOPE + MLA_ROPE)
            wqb_n = wqb[:, :, :MLA_NOPE].reshape(qrank, hds * MLA_NOPE).astype(BF16)
            wqb_r = wqb[:, :, MLA_NOPE:].reshape(qrank, hds * MLA_ROPE).astype(BF16)
            qscale = (MLA_NOPE + MLA_ROPE) ** -0.5 * math.log2(math.e)

            def epi_qn(accs, i, j, ec, em, er, et, qscale=qscale):
                return (accs[0] * qscale,)

            (q_nope,) = fused_mm(
                name="mla_q_nope", m=t, k=qrank, n=hds * MLA_NOPE, tm=tm, tn=_pick_tile(hds * MLA_NOPE, 1024),
                slab=slab, xs=[(qn, 0)], prologue=_pro_cast, ws=[(wqb_n, 0)], epilogue=epi_qn,
                outs=[(hds * MLA_NOPE, BF16, _pick_tile(hds * MLA_NOPE, 1024), lambda j: j)])

            def epi_qr(accs, i, j, ec, em, er, et, tiles=slab // tm, qscale=qscale):
                a = accs[0]
                return (jnp.where(i >= tiles, _rope_rot(a, et[0], et[1]), a) * qscale,)

            (q_rope,) = fused_mm(
                name="mla_q_rope", m=t, k=qrank, n=hds * MLA_ROPE, tm=tm, tn=hds * MLA_ROPE,
                slab=slab, xs=[(qn, 0)], prologue=_pro_cast, ws=[(wqb_r, 0)], epilogue=epi_qr,
                epi_tabs=[cos_t, sin_t], outs=[(hds * MLA_ROPE, BF16, hds * MLA_ROPE, lambda j: 0)])
            ckv_s = ckv[slab:].reshape(bs, ns, kvrank)
            kr_s = kr[slab:, :MLA_ROPE].reshape(bs, ns, MLA_ROPE)
            ckv_all = jnp.concatenate(
                [jnp.concatenate([ckv_s, cache_mla_ckv[:, r]], axis=1).reshape(-1, kvrank), ckv[:slab]],
                axis=0).astype(BF16)
            kr_all = jnp.concatenate(
                [jnp.concatenate([kr_s, cache_mla_krope[:, r]], axis=1).reshape(-1, MLA_ROPE),
                 kr[:slab, :MLA_ROPE]], axis=0).astype(BF16)
            wkvb = mla_wkv_b[r].reshape(kvrank, hds, MLA_NOPE + MLA_V)
            wkvb_k = wkvb[:, :, :MLA_NOPE].reshape(kvrank, hds * MLA_NOPE).astype(BF16)
            wkvb_vt = wkvb[:, :, MLA_NOPE:].reshape(kvrank, hds * MLA_V).T.astype(BF16)
            tk_rows = ckv_all.shape[0]
            tm_kv = _pick_tile(tk_rows, 1024)
            (k_nope,) = fused_mm(
                name="mla_k_expand", m=tk_rows, k=kvrank, n=hds * MLA_NOPE, tm=tm_kv,
                tn=_pick_tile(hds * MLA_NOPE, 1024), slab=tm_kv, xs=[(ckv_all, 0)], prologue=_pro_cast,
                ws=[(wkvb_k, 0)], epilogue=_epi_plain,
                outs=[(hds * MLA_NOPE, BF16, _pick_tile(hds * MLA_NOPE, 1024), lambda j: j)])
            v_t = mm_nt(wkvb_vt, ckv_all, tm=_pick_tile(tk_rows, 512), name="mla_v_expand_t")
            k_prompt0 = bs * (ns + past)
            att = mla_attention(q_nope, q_rope, k_nope, v_t, kr_all, n_heads=hds, batch=bp, nq_rows=sp,
                                nk_rows=sp, q_row0=0, k_row0=k_prompt0, tq=sp, tk=sp)
            tqs = _pick_tile(ns, 512)
            tks = next(c for c in (1536, 1152, 1024, 768, 512, 384, 256, 128) if (ns + past) % c == 0)
            att = mla_attention(q_nope, q_rope, k_nope, v_t, kr_all, n_heads=hds, batch=bs, nq_rows=ns,
                                nk_rows=ns + past, q_row0=slab, k_row0=0, tq=tqs, tk=tks, prev_out=att)
            (x,) = fused_mm(
                name="mla_out_proj", m=t, k=hds * MLA_V, n=d, tm=tm, tn=tn, slab=slab,
                xs=[(att, 0)], prologue=_pro_cast, ws=[(mla_wo[r].astype(BF16), 0)],
                epi_mods=[gt_mix], epi_rows=[x], epilogue=_epi_residual,
                outs=[(d, F32, tn, lambda j: j)])
        elif kind == 1:
            hq, hkv, hd = GQA_HEADS, GQA_KV_HEADS, GQA_HEAD_DIM
            wqkv = jnp.concatenate([gqa_wq[r], gqa_wk[r], gqa_wv[r]], axis=1).astype(BF16)
            nqkv = wqkv.shape[1]
            tn_qkv = hkv * hd
            n_rope_tiles = (hq * hd + hkv * hd) // tn_qkv
            n_q_tiles = (hq * hd) // tn_qkv

            qscale = hd ** -0.5 * math.log2(math.e)

            def epi_qkv(accs, i, j, ec, em, er, et, tiles=slab // tmh, n_rope_tiles=n_rope_tiles,
                        n_q_tiles=n_q_tiles, qscale=qscale):
                a = accs[0]
                roped = jnp.where((i >= tiles) & (j < n_rope_tiles), _rope_rot(a, et[0], et[1]), a)
                return roped * jnp.where(j < n_q_tiles, qscale, 1.0), a

            qkv, kv_f32 = fused_mm(
                name="gqa_qkv_proj", m=t, k=d, n=nqkv, tm=tmh, tn=tn_qkv, slab=slab,
                xs=[(x, 0)], pro_consts=[g_mix], pro_mods=[sh_mix, sc_mix], prologue=_pro_adaln,
                ws=[(wqkv, 0)], epi_tabs=[cos_t, sin_t], epilogue=epi_qkv,
                outs=[(nqkv, BF16, tn_qkv, lambda j: j),
                      (2 * tn_qkv, F32, tn_qkv, functools.partial(lambda j, nq: jnp.maximum(j - nq, 0), nq=n_q_tiles))])
            outs['gqa_k'] = kv_f32[:slab, :hkv * hd].reshape(bp, 1, sp, hkv, hd)
            outs['gqa_v'] = kv_f32[:slab, hkv * hd:].reshape(bp, 1, sp, hkv, hd)
            att = gqa_attention(qkv, gqa_sink[r], batch=bp, n_rows=sp, q_row0=0, window=False)
            ctx_k = cache_gqa_k[:, r].reshape(bs, past, hkv * hd).astype(BF16)
            ctx_v = cache_gqa_v[:, r].reshape(bs, past, hkv * hd).astype(BF16)
            att = gqa_attention(qkv, gqa_sink[r], batch=bs, n_rows=ns, q_row0=slab, window=True,
                                ctx_k=ctx_k, ctx_v=ctx_v, prev_out=att)
            (x,) = fused_mm(
                name="gqa_out_proj", m=t, k=hq * hd, n=d, tm=tm, tn=tn, slab=slab,
                xs=[(att, 0)], prologue=_pro_cast, ws=[(gqa_wo[r].astype(BF16), 0)],
                epi_mods=[gt_mix], epi_rows=[x], epilogue=_epi_residual,
                outs=[(d, F32, tn, lambda j: j)])
        elif kind == 2:
            grp, hch, pst, lch = d // S5_GROUP_CH, S5_GROUP_CH, S5_STATE, S5_SCAN_CHUNK
            hf = norm_rows(x, g_mix, tm=tmh, slab=slab, out_dtype=F32, shift=sh_mix, scale=sc_mix, name="adaln_s5")
            w1, w2, coef = s5_weights(s5_lam_re[r], s5_lam_im[r], s5_b_re[r], s5_b_im[r],
                                      s5_c_re[r], s5_c_im[r], s5_log_dt[r])

            sel, unsel = _s5_selectors(lch, hch)
            cp, cs = sp // lch, ns // lch
            rs = _pick_tile(cs, 128)
            sb = math.gcd(bp, 8)
            u_p = s5_pack(hf, sel, row0=0, nbatch=bp, nseq=sp, rows_blk=cp, seqs_blk=sb)
            u_s = s5_pack(hf, sel, row0=slab, nbatch=bs, nseq=ns, rows_blk=rs)
            h0_p = jnp.zeros((grp, 2, bp, 2 * pst), F32)
            h0_s = state_s5[:, r].transpose(2, 1, 0, 4, 3).reshape(grp, 2, bs, 2 * pst)
            gb = 2 if grp % 2 == 0 else 1
            y_p, fin_p = s5_scan(u_p.reshape(grp, cp * bp, lch * hch), w1, w2, coef, h0_p, gb=gb, nb=bp)
            y_s, _ = s5_scan(u_s.reshape(grp, cs * bs, lch * hch), w1, w2, coef, h0_s, gb=gb, nb=bs)
            outs['s5'] = fin_p.reshape(grp, 2, bp, 2, pst).transpose(2, 1, 0, 4, 3)[:, None]
            y = s5_unpack(y_p.reshape(grp, cp, bp, lch * hch), unsel, t=t, row0=0, rows_blk=cp, seqs_blk=sb)
            y = s5_unpack(y_s.reshape(grp, cs, bs, lch * hch), unsel, t=t, row0=slab, rows_blk=rs, prev_out=y)
            wg = s5_w_glu[r].astype(BF16)
            bg = s5_b_glu[r][None]
            (x,) = fused_mm(
                name="s5_glu", m=t, k=d, n=d, tm=tmh, tn=tn, slab=slab,
                xs=[(y, 0), (hf, 0)], pro_consts=[s5_d[r][None]], prologue=_pro_s5_post,
                ws=[(wg, 0), (wg, d // tn)], epi_cols=[(bg, 0), (bg, d // tn)],
                epi_mods=[gt_mix], epi_rows=[x], epilogue=_epi_glu_residual,
                outs=[(d, F32, tn, lambda j: j)])
        else:
            w1c = conv_w_pw1[r].astype(BF16)
            b1c = conv_b_pw1[r][None]
            (z,) = fused_mm(
                name="conv_pw1_glu", m=t, k=d, n=d, tm=tmh, tn=tn, slab=slab,
                xs=[(x, 0)], pro_consts=[g_mix], pro_mods=[sh_mix, sc_mix], prologue=_pro_adaln,
                ws=[(w1c, 0), (w1c, d // tn)], epi_cols=[(b1c, 0), (b1c, d // tn)], epilogue=_epi_glu,
                outs=[(d, F32, tn, lambda j: j)])
            zc = dwconv(z, conv_w_dw[r], conv_b_dw[r][None], tile=sp, cb=tn, slab=slab, seq_prompt=sp)
            (x,) = fused_mm(
                name="conv_pw2", m=t, k=d, n=d, tm=tmh, tn=tn, slab=slab,
                xs=[(zc, 0)], pro_consts=[conv_ln_g[r][None], conv_ln_b[r][None]], prologue=_pro_ln_silu,
                ws=[(conv_w_pw2[r].astype(BF16), 0)], epi_cols=[(conv_b_pw2[r][None], 0)],
                epi_mods=[gt_mix], epi_rows=[x], epilogue=_epi_bias_residual,
                outs=[(d, F32, tn, lambda j: j)])
        x = moe_layer(x, norm_ffn_g[layer][None], sh_ffn, sc_ffn, gt_ffn, (w_r_hi, w_r_lo), moe_b_router,
                      moe_w1[layer], moe_w3[layer], moe_w2[layer], tm_rows=tmh, tm_moe=256, slab=slab)

    y = norm_rows(x, final_norm_g[None], tm=tmh, slab=slab, out_dtype=F32, name="final_norm")
    return (y[:slab].reshape(bp, sp, d), y[slab:].reshape(bs, ns, d),
            outs['ckv'], outs['krope'], outs['gqa_k'], outs['gqa_v'], outs['s5'])
```

```python
import functools
import math

import jax
import jax.numpy as jnp
from jax import lax
from jax.experimental import pallas as pl
from jax.experimental.pallas import tpu as pltpu

GRID_W = 64
NORM_EPS = 1e-6
ROPE_BASE = 10000.0
NEG_INF = -1e30

MLA_HEADS = 16
MLA_NOPE = 128
MLA_ROPE = 64
MLA_V = 128

GQA_HEADS = 32
GQA_KV_HEADS = 8
GQA_HEAD_DIM = 64
WINDOW = 128
BAND_BLOCK = 128

S5_GROUP_CH = 16
S5_STATE = 64
S5_SCAN_CHUNK = 16

CONV_WIDTH = 31
CONV_HALO = 16

N_EXPERT_GROUPS = 4
TOP_K = 2

LANES = 128
VMEM_LIMIT_BYTES = 56 * 1024 * 1024

BF16 = jnp.bfloat16
F32 = jnp.float32


def _cparams(sem):
    return pltpu.CompilerParams(dimension_semantics=sem, vmem_limit_bytes=VMEM_LIMIT_BYTES)


def _silu(x):
    return x * jax.nn.sigmoid(x)


def _gelu_tanh(x):
    return 0.5 * x * (1.0 + jnp.tanh(math.sqrt(2.0 / math.pi) * (x + 0.044715 * (x * x * x))))


def _rms(x, g):
    return (x * lax.rsqrt(jnp.mean(x * x, axis=-1, keepdims=True) + NORM_EPS)) * g


def _rope_rot(x, cos_t, sin_t):
    w = x.shape[-1]
    reps = w // LANES
    c = jnp.concatenate([cos_t] * reps, axis=1) if reps > 1 else cos_t
    s = jnp.concatenate([sin_t] * reps, axis=1) if reps > 1 else sin_t
    lane = lax.broadcasted_iota(jnp.int32, x.shape, 1)
    first_half = (lane % 64) < 32
    swapped = jnp.where(first_half, pltpu.roll(x, w - 32, 1), pltpu.roll(x, 32, 1))
    return x * c + swapped * s


def _fused_mm_body(*refs, n_x, n_pc, n_pm, n_w, n_ec, n_ek, n_em, n_er, n_et, n_out, prologue, epilogue):
    pos = 0

    def take(n):
        nonlocal pos
        r = refs[pos:pos + n]
        pos += n
        return r

    x_refs = take(n_x)
    pc_refs = take(n_pc)
    pm_refs = take(n_pm)
    w_refs = take(n_w)
    ec_refs = take(n_ec + n_ek)
    em_refs = take(n_em)
    er_refs = take(n_er)
    et_refs = take(n_et)
    out_refs = take(n_out)
    (h_ref,) = take(1)
    i = pl.program_id(0)
    j = pl.program_id(1)

    @pl.when(j == 0)
    def _():
        h = prologue([r[...] for r in x_refs], [r[...] for r in pc_refs], [r[0] for r in pm_refs])
        h_ref[...] = h.astype(h_ref.dtype)

    h = h_ref[...]
    accs = [jnp.dot(h, w_ref[...].astype(BF16), preferred_element_type=F32) for w_ref in w_refs]
    outs = epilogue(accs, i, j, [r[...] for r in ec_refs], [r[0] for r in em_refs],
                    [r[...] for r in er_refs], [r[...] for r in et_refs])
    for o_ref, o in zip(out_refs, outs):
        o_ref[...] = o.astype(o_ref.dtype)


def fused_mm(*, name, m, k, n, tm, tn, slab, xs, prologue, ws, epilogue, outs,
             pro_consts=(), pro_mods=(), epi_cols=(), epi_consts=(), epi_mods=(), epi_rows=(), epi_tabs=()):
    assert m % tm == 0 and n % tn == 0 and slab % tm == 0
    grid = (m // tm, n // tn)
    tiles_per_slab = slab // tm
    in_specs, args = [], []
    for arr, off in xs:
        in_specs.append(pl.BlockSpec((tm, k), functools.partial(lambda i, j, off: (i + off, 0), off=off)))
        args.append(arr)
    for arr in pro_consts:
        in_specs.append(pl.BlockSpec((1, k), lambda i, j: (0, 0)))
        args.append(arr)
    for arr in pro_mods:
        in_specs.append(pl.BlockSpec((1, 1, k), lambda i, j: (i // tiles_per_slab, 0, 0)))
        args.append(arr)
    for arr, off in ws:
        in_specs.append(pl.BlockSpec((k, tn), functools.partial(lambda i, j, off: (0, j + off), off=off)))
        args.append(arr)
    for arr, off in epi_cols:
        in_specs.append(pl.BlockSpec((1, tn), functools.partial(lambda i, j, off: (0, j + off), off=off)))
        args.append(arr)
    for arr in epi_consts:
        in_specs.append(pl.BlockSpec(arr.shape, lambda i, j: (0, 0)))
        args.append(arr)
    for arr in epi_mods:
        in_specs.append(pl.BlockSpec((1, 1, tn), lambda i, j: (i // tiles_per_slab, 0, j)))
        args.append(arr)
    for arr in epi_rows:
        in_specs.append(pl.BlockSpec((tm, tn), lambda i, j: (i, j)))
        args.append(arr)
    for arr in epi_tabs:
        in_specs.append(pl.BlockSpec((tm, arr.shape[1]), lambda i, j: (i % tiles_per_slab, 0)))
        args.append(arr)
    out_specs, out_shapes = [], []
    for cols, dtype, bcols, cfn in outs:
        out_specs.append(pl.BlockSpec((tm, bcols), functools.partial(lambda i, j, cfn: (i, cfn(j)), cfn=cfn)))
        out_shapes.append(jax.ShapeDtypeStruct((m, cols), dtype))
    body = functools.partial(
        _fused_mm_body, n_x=len(xs), n_pc=len(pro_consts), n_pm=len(pro_mods), n_w=len(ws),
        n_ec=len(epi_cols), n_ek=len(epi_consts), n_em=len(epi_mods), n_er=len(epi_rows), n_et=len(epi_tabs),
        n_out=len(outs), prologue=prologue, epilogue=epilogue)
    res = pl.pallas_call(
        body, grid=grid, in_specs=in_specs, out_specs=out_specs, out_shape=out_shapes,
        scratch_shapes=[pltpu.VMEM((tm, k), BF16)],
        compiler_params=_cparams(("parallel", "arbitrary")), name=name)(*args)
    return res


def _pro_cast(xv, cv, mv):
    return xv[0]


def _pro_adaln(xv, cv, mv):
    return _rms(xv[0], cv[0]) * (1.0 + mv[1]) + mv[0]


def _pro_rms(xv, cv, mv):
    return _rms(xv[0], cv[0])


def _pro_ln_silu(xv, cv, mv):
    x = xv[0]
    mu = jnp.mean(x, axis=-1, keepdims=True)
    xc = x - mu
    var = jnp.mean(xc * xc, axis=-1, keepdims=True)
    return _silu((xc * lax.rsqrt(var + NORM_EPS)) * cv[0] + cv[1])


def _pro_s5_post(xv, cv, mv):
    return _gelu_tanh(xv[0] + cv[0] * xv[1])


def _epi_plain(accs, i, j, ec, em, er, et):
    return (accs[0],)


def _epi_bias_residual(accs, i, j, ec, em, er, et):
    return (er[0] + em[0] * (accs[0] + ec[0]),)


def _epi_residual(accs, i, j, ec, em, er, et):
    return (er[0] + em[0] * accs[0],)


def _epi_glu(accs, i, j, ec, em, er, et):
    return ((accs[0] + ec[0]) * jax.nn.sigmoid(accs[1] + ec[1]),)


def _epi_glu_residual(accs, i, j, ec, em, er, et):
    return (er[0] + em[0] * ((accs[0] + ec[0]) * jax.nn.sigmoid(accs[1] + ec[1])),)


def _modulation_body(c_ref, w_ref, b_ref, o_ref):
    h = _silu(c_ref[...]).astype(BF16)
    o_ref[0] = jnp.dot(h, w_ref[0].astype(BF16), preferred_element_type=F32) + b_ref[0]


def modulation_all(cond, ada_w, ada_b, tn):
    depth, d, n = ada_w.shape
    rows = cond.shape[0]
    return pl.pallas_call(
        _modulation_body, grid=(depth, n // tn),
        in_specs=[pl.BlockSpec((rows, d), lambda l, j: (0, 0)),
                  pl.BlockSpec((1, d, tn), lambda l, j: (l, 0, j)),
                  pl.BlockSpec((1, 1, tn), lambda l, j: (l, 0, j))],
        out_specs=pl.BlockSpec((1, rows, tn), lambda l, j: (l, 0, j)),
        out_shape=jax.ShapeDtypeStruct((depth, rows, n), F32),
        compiler_params=_cparams(("parallel", "parallel")), name="modulation")(
            cond, ada_w, ada_b.reshape(depth, 1, n))


def _norm_body(*refs, modulated):
    x_ref, g_ref = refs[0], refs[1]
    y = _rms(x_ref[...], g_ref[...])
    if modulated:
        shift_ref, scale_ref = refs[2], refs[3]
        y = y * (1.0 + scale_ref[0]) + shift_ref[0]
    o_ref = refs[-1]
    o_ref[...] = y.astype(o_ref.dtype)


def norm_rows(x, g, *, tm, slab, out_dtype, shift=None, scale=None, name="norm"):
    m, d = x.shape
    tiles_per_slab = slab // tm
    in_specs = [pl.BlockSpec((tm, d), lambda i: (i, 0)), pl.BlockSpec((1, d), lambda i: (0, 0))]
    args = [x, g]
    modulated = shift is not None
    if modulated:
        in_specs += [pl.BlockSpec((1, 1, d), lambda i: (i // tiles_per_slab, 0, 0))] * 2
        args += [shift, scale]
    return pl.pallas_call(
        functools.partial(_norm_body, modulated=modulated),
        grid=(m // tm,), in_specs=in_specs, out_specs=pl.BlockSpec((tm, d), lambda i: (i, 0)),
        out_shape=jax.ShapeDtypeStruct((m, d), out_dtype),
        compiler_params=_cparams(("parallel",)), name=name)(*args)


def _route_body(x_ref, g_ref, shift_ref, scale_ref, whi_ref, wlo_ref, b_ref, tri_ref,
                h_ref, idx_ref, gate_ref, cnt_ref, carry_ref, *, n_groups):
    @pl.when(pl.program_id(0) == 0)
    def _():
        carry_ref[...] = jnp.zeros(carry_ref.shape, F32)

    y = _rms(x_ref[...], g_ref[...]) * (1.0 + scale_ref[0]) + shift_ref[0]
    y_hi = y.astype(BF16)
    h_ref[...] = y_hi
    y_lo = (y - y_hi.astype(F32)).astype(BF16)
    nt = (((1,), (1,)), ((), ()))
    w_hi = whi_ref[...]
    logits = (lax.dot_general(w_hi, y_hi, nt, preferred_element_type=F32)
              + lax.dot_general(w_hi, y_lo, nt, preferred_element_type=F32)
              + lax.dot_general(wlo_ref[...], y_hi, nt, preferred_element_type=F32))
    n_exp, tm = logits.shape
    epg = n_exp // n_groups
    scores = jax.nn.sigmoid(logits)
    biased = scores + b_ref[...]
    sub = lax.broadcasted_iota(jnp.int32, (epg, tm), 0).astype(F32)
    best = sel = loc1 = loc2 = None
    for grp in range(n_groups):
        blk = biased[grp * epg:(grp + 1) * epg]
        m1 = jnp.max(blk, axis=0, keepdims=True)
        i1 = jnp.min(jnp.where(blk == m1, sub, float(epg)), axis=0, keepdims=True)
        rest = jnp.where(sub == i1, -jnp.inf, blk)
        m2 = jnp.max(rest, axis=0, keepdims=True)
        i2 = jnp.min(jnp.where(rest == m2, sub, float(epg)), axis=0, keepdims=True)
        gsum = m1 + m2
        if grp == 0:
            best, sel, loc1, loc2 = gsum, jnp.zeros_like(gsum), i1, i2
        else:
            better = gsum > best
            best = jnp.where(better, gsum, best)
            sel = jnp.where(better, float(grp), sel)
            loc1 = jnp.where(better, i1, loc1)
            loc2 = jnp.where(better, i2, loc2)
    e1 = sel * epg + loc1
    e2 = sel * epg + loc2
    row = lax.broadcasted_iota(jnp.int32, (n_exp, tm), 0).astype(F32)
    hit1, hit2 = row == e1, row == e2
    s1 = jnp.sum(jnp.where(hit1, scores, 0.0), axis=0, keepdims=True)
    s2 = jnp.sum(jnp.where(hit2, scores, 0.0), axis=0, keepdims=True)
    total = s1 + s2
    onehot = jnp.where(hit1, 1.0, 0.0) + jnp.where(hit2, 1.0, 0.0)
    csum = jnp.dot(onehot.astype(BF16), tri_ref[...], preferred_element_type=F32)
    before = carry_ref[...] + csum - onehot
    r1 = jnp.sum(jnp.where(hit1, before, 0.0), axis=0, keepdims=True)
    r2 = jnp.sum(jnp.where(hit2, before, 0.0), axis=0, keepdims=True)
    carry = carry_ref[...] + csum[:, tm - 1:tm]
    carry_ref[...] = carry
    cnt_ref[...] = jnp.broadcast_to(carry, cnt_ref.shape)
    idx_ref[...] = jnp.concatenate([e1, e2, r1, r2, jnp.zeros((4, tm), F32)], axis=0).astype(jnp.int32)
    gate_ref[...] = jnp.concatenate([s1 / total, s2 / total, jnp.zeros((6, tm), F32)], axis=0)


def route(x, g, shift, scale, wt_hi, wt_lo, b_router, *, tm, slab):
    m, d = x.shape
    n_exp = wt_hi.shape[0]
    tiles_per_slab = slab // tm
    tri = jnp.triu(jnp.ones((tm, tm), BF16))
    const = lambda i: (0, 0)
    return pl.pallas_call(
        functools.partial(_route_body, n_groups=N_EXPERT_GROUPS),
        grid=(m // tm,),
        in_specs=[pl.BlockSpec((tm, d), lambda i: (i, 0)), pl.BlockSpec((1, d), const),
                  pl.BlockSpec((1, 1, d), lambda i: (i // tiles_per_slab, 0, 0)),
                  pl.BlockSpec((1, 1, d), lambda i: (i // tiles_per_slab, 0, 0)),
                  pl.BlockSpec((n_exp, d), const), pl.BlockSpec((n_exp, d), const),
                  pl.BlockSpec((n_exp, 1), const), pl.BlockSpec((tm, tm), const)],
        out_specs=[pl.BlockSpec((tm, d), lambda i: (i, 0)), pl.BlockSpec((8, tm), lambda i: (0, i)),
                   pl.BlockSpec((8, tm), lambda i: (0, i)), pl.BlockSpec((n_exp, LANES), const)],
        out_shape=[jax.ShapeDtypeStruct((m, d), BF16), jax.ShapeDtypeStruct((8, m), jnp.int32),
                   jax.ShapeDtypeStruct((8, m), F32), jax.ShapeDtypeStruct((n_exp, LANES), F32)],
        scratch_shapes=[pltpu.VMEM((n_exp, 1), F32)],
        compiler_params=_cparams(("arbitrary",)), name="adaln_route")(
            x, g, shift, scale, wt_hi, wt_lo, b_router.reshape(n_exp, 1), tri)


def _mla_attn_body(*refs, nk, nh, aliased):
    if aliased:
        refs = refs[1:]
    qn_ref, qr_ref, kn_ref, vt_ref, kr_ref, o_ref, m_ref, l_ref, acc_ref = refs
    ki = pl.program_id(3)

    @pl.when(ki == 0)
    def _():
        m_ref[...] = jnp.full(m_ref.shape, -jnp.inf, F32)
        l_ref[...] = jnp.zeros(l_ref.shape, F32)
        acc_ref[...] = jnp.zeros(acc_ref.shape, F32)

    kr = kr_ref[...]
    scores, probs = {}, {}

    def qk(hh):
        q = jnp.concatenate([qn_ref[:, hh * MLA_NOPE:(hh + 1) * MLA_NOPE],
                             qr_ref[:, hh * MLA_ROPE:(hh + 1) * MLA_ROPE]], axis=1)
        k = jnp.concatenate([kn_ref[:, hh * MLA_NOPE:(hh + 1) * MLA_NOPE], kr], axis=1)
        scores[hh] = lax.dot_general(k, q, (((1,), (1,)), ((), ())), preferred_element_type=F32)

    def softmax(hh):
        st = scores.pop(hh)
        m_prev = m_ref[hh]
        m_new = jnp.maximum(m_prev, jnp.max(st, axis=0, keepdims=True))
        alpha = jnp.exp2(m_prev - m_new)
        p = jnp.exp2(st - m_new)
        l_ref[hh] = alpha * l_ref[hh] + jnp.sum(p, axis=0, keepdims=True)
        m_ref[hh] = m_new
        probs[hh] = (alpha, p.astype(BF16))

    def pv(hh):
        alpha, p = probs.pop(hh)
        acc_ref[hh] = alpha * acc_ref[hh] + jnp.dot(
            vt_ref[hh * MLA_V:(hh + 1) * MLA_V, :], p, preferred_element_type=F32)

    for t in range(nh + 2):
        if t < nh:
            qk(t)
        if 0 <= t - 1 < nh:
            softmax(t - 1)
        if 0 <= t - 2 < nh:
            pv(t - 2)

    @pl.when(ki == nk - 1)
    def _():
        o_ref[...] = jnp.concatenate([(acc_ref[hh] / l_ref[hh]).T for hh in range(nh)],
                                     axis=1).astype(o_ref.dtype)


def mla_attention(qn, qr, kn, vt, kr, *, n_heads, batch, nq_rows, nk_rows, q_row0, k_row0, tq, tk, prev_out=None):
    t = qn.shape[0]
    nh = min(4, n_heads)
    hg = n_heads // nh
    nq, nk = nq_rows // tq, nk_rows // tk
    assert q_row0 % tq == 0 and k_row0 % tk == 0 and nq_rows % tq == 0 and nk_rows % tk == 0
    qb0, kb0 = q_row0 // tq, k_row0 // tk

    def qmap(b, h, qi, ki):
        return (qb0 + b * nq + qi, h)

    def kmap(b, h, qi, ki):
        return (kb0 + b * nk + ki, h)

    in_specs = [pl.BlockSpec((tq, nh * MLA_NOPE), qmap),
                pl.BlockSpec((tq, nh * MLA_ROPE), qmap),
                pl.BlockSpec((tk, nh * MLA_NOPE), kmap),
                pl.BlockSpec((nh * MLA_V, tk), lambda b, h, qi, ki: (h, kb0 + b * nk + ki)),
                pl.BlockSpec((tk, MLA_ROPE), lambda b, h, qi, ki: (kb0 + b * nk + ki, 0))]
    args = [qn, qr, kn, vt, kr]
    aliases = {}
    if prev_out is not None:
        in_specs = [pl.BlockSpec(memory_space=pl.ANY)] + in_specs
        args = [prev_out] + args
        aliases = {0: 0}
    return pl.pallas_call(
        functools.partial(_mla_attn_body, nk=nk, nh=nh, aliased=prev_out is not None),
        grid=(batch, hg, nq, nk), in_specs=in_specs,
        out_specs=pl.BlockSpec((tq, nh * MLA_V), qmap),
        out_shape=jax.ShapeDtypeStruct((t, n_heads * MLA_V), BF16),
        scratch_shapes=[pltpu.VMEM((nh, 1, tq), F32), pltpu.VMEM((nh, 1, tq), F32),
                        pltpu.VMEM((nh, MLA_V, tq), F32)],
        input_output_aliases=aliases,
        compiler_params=_cparams(("parallel", "parallel", "parallel", "arbitrary")),
        name="mla_attention")(*args)


def _mm_nt_body(w_ref, x_ref, o_ref):
    o_ref[...] = lax.dot_general(w_ref[...], x_ref[...], (((1,), (1,)), ((), ())),
                                 preferred_element_type=F32).astype(o_ref.dtype)


def mm_nt(wt, x, *, tm, name):
    n, k = wt.shape
    m = x.shape[0]
    return pl.pallas_call(
        _mm_nt_body, grid=(m // tm,),
        in_specs=[pl.BlockSpec((n, k), lambda i: (0, 0)), pl.BlockSpec((tm, k), lambda i: (i, 0))],
        out_specs=pl.BlockSpec((n, tm), lambda i: (0, i)),
        out_shape=jax.ShapeDtypeStruct((n, m), BF16),
        compiler_params=_cparams(("parallel",)), name=name)(wt, x)


def _gqa_attn_body(*refs, n_seg, window, nq, aliased, group):
    if aliased:
        refs = refs[1:]
    sink_ref, q_ref = refs[0], refs[1]
    k_refs = refs[2:2 + n_seg]
    v_refs = refs[2 + n_seg:2 + 2 * n_seg]
    o_ref = refs[2 + 2 * n_seg]
    p_idx = pl.program_id(1)
    qi = pl.program_id(2)
    d = GQA_HEAD_DIM
    q = q_ref[...]
    tq = q.shape[0]
    ks = [r[...] for r in k_refs]
    vs = [r[...] for r in v_refs]
    if ks[-1].ndim == 3:
        ks[-1] = ks[-1][0]
        vs[-1] = vs[-1][0]
    n_loc = 3 * BAND_BLOCK
    if window:
        key = lax.broadcasted_iota(jnp.int32, (n_loc, tq), 0)
        qry = lax.broadcasted_iota(jnp.int32, (n_loc, tq), 1)
        lo = jnp.where(qi > 0, 0, BAND_BLOCK)
        hi = jnp.where(qi < nq - 1, n_loc, 2 * BAND_BLOCK)
        valid1 = (jnp.abs(qry + BAND_BLOCK - key) <= WINDOW) & (key >= lo) & (key < hi)
        valid = jnp.concatenate([valid1.astype(F32)] * group, axis=1) > 0.5
    log2e = math.log2(math.e)
    o_t = []
    for kvh in range(2):
        k = jnp.concatenate([kk[:, kvh * d:(kvh + 1) * d] for kk in ks], axis=0)
        v = jnp.concatenate([vv[:, kvh * d:(kvh + 1) * d] for vv in vs], axis=0)
        base = kvh * group * d
        q4 = jnp.concatenate([q[:, base + g * d: base + (g + 1) * d] for g in range(group)], axis=0)
        s = lax.dot_general(k, q4, (((1,), (1,)), ((), ())), preferred_element_type=F32)
        head0 = (p_idx * 2 + kvh) * group
        snk = jnp.concatenate([jnp.full((1, tq), sink_ref[head0 + g] * log2e, F32) for g in range(group)], axis=1)
        if window:
            s_loc = jnp.where(valid, s[:n_loc], NEG_INF)
            s_ctx = s[n_loc:]
            m = jnp.maximum(jnp.maximum(jnp.max(s_loc, axis=0, keepdims=True),
                                        jnp.max(s_ctx, axis=0, keepdims=True)), snk)
            p = jnp.concatenate([jnp.exp2(s_loc - m), jnp.exp2(s_ctx - m)], axis=0)
        else:
            m = jnp.maximum(jnp.max(s, axis=0, keepdims=True), snk)
            p = jnp.exp2(s - m)
        denom = jnp.sum(p, axis=0, keepdims=True) + jnp.exp2(snk - m)
        o_t.append(lax.dot_general(v, p.astype(BF16), (((0,), (0,)), ((), ())),
                                   preferred_element_type=F32) / denom)
    o = jnp.concatenate(o_t, axis=0).T
    pieces = [o[g * tq:(g + 1) * tq, kvh * d:(kvh + 1) * d] for kvh in range(2) for g in range(group)]
    o_ref[...] = jnp.concatenate(pieces, axis=1).astype(o_ref.dtype)


def gqa_attention(qkv, sink, *, batch, n_rows, q_row0, window, ctx_k=None, ctx_v=None, prev_out=None):
    t = qkv.shape[0]
    d, h, kvh = GQA_HEAD_DIM, GQA_HEADS, GQA_KV_HEADS
    group = h // kvh
    pairs = kvh // 2
    qw = 2 * group * d
    kcol0 = (h * d) // LANES
    vcol0 = (h * d + kvh * d) // LANES
    tq = BAND_BLOCK if window else n_rows
    nq = n_rows // tq
    qb0 = q_row0 // tq
    assert q_row0 % tq == 0

    def qmap(b, p, i):
        return (qb0 + b * nq + i, p)

    in_specs = [pl.BlockSpec(memory_space=pltpu.SMEM), pl.BlockSpec((tq, qw), qmap)]
    args = [sink, qkv]
    if window:
        def kv_specs(col0):
            return [
                pl.BlockSpec((tq, LANES), lambda b, p, i: (qb0 + b * nq + jnp.maximum(i - 1, 0), col0 + p)),
                pl.BlockSpec((tq, LANES), lambda b, p, i: (qb0 + b * nq + i, col0 + p)),
                pl.BlockSpec((tq, LANES), lambda b, p, i: (qb0 + b * nq + jnp.minimum(i + 1, nq - 1), col0 + p)),
            ]
        n_ctx = ctx_k.shape[1]
        ctx_spec = pl.BlockSpec((1, n_ctx, LANES), lambda b, p, i: (b, 0, p))
        in_specs += kv_specs(kcol0) + [ctx_spec] + kv_specs(vcol0) + [ctx_spec]
        args += [qkv, qkv, qkv, ctx_k, qkv, qkv, qkv, ctx_v]
        n_seg = 4
    else:
        in_specs += [pl.BlockSpec((tq, LANES), lambda b, p, i: (qb0 + b * nq + i, kcol0 + p)),
                     pl.BlockSpec((tq, LANES), lambda b, p, i: (qb0 + b * nq + i, vcol0 + p))]
        args += [qkv, qkv]
        n_seg = 1
    aliases = {}
    if prev_out is not None:
        in_specs = [pl.BlockSpec(memory_space=pl.ANY)] + in_specs
        args = [prev_out] + args
        aliases = {0: 0}
    return pl.pallas_call(
        functools.partial(_gqa_attn_body, n_seg=n_seg, window=window, nq=nq,
                          aliased=prev_out is not None, group=group),
        grid=(batch, pairs, nq), in_specs=in_specs,
        out_specs=pl.BlockSpec((tq, qw), qmap),
        out_shape=jax.ShapeDtypeStruct((t, h * d), BF16),
        input_output_aliases=aliases,
        compiler_params=_cparams(("parallel", "parallel", "parallel")),
        name="gqa_window_attention" if window else "gqa_dense_attention")(*args)


def _s5_body(u_ref, w1_ref, w2_ref, coef_ref, h0_ref, y_ref, fin_ref, z_ref, sp_ref, *, gb, nb, n_steps):
    lc = S5_SCAN_CHUNK * S5_GROUP_CH
    ns = 2 * S5_STATE
    for g in range(gb):
        for b in range(nb):
            z = jnp.dot(u_ref[g, :, b * lc:(b + 1) * lc], w1_ref[g], preferred_element_type=F32)
            for k in range(4):
                z_ref[g, k, pl.ds(b, n_steps, stride=nb), :] = z[:, k * ns:(k + 1) * ns]

    coef = coef_ref[...]

    def step(c, carry):
        new = []
        rf = pl.multiple_of(c * nb, nb)
        rb = pl.multiple_of((n_steps - 1 - c) * nb, nb)
        for g in range(gb):
            for dirn, rows in ((0, rf), (1, rb)):
                s, tw = carry[4 * g + 2 * dirn], carry[4 * g + 2 * dirn + 1]
                a1 = coef[g, 2 * dirn:2 * dirn + 1, :]
                a2 = coef[g, 2 * dirn + 1:2 * dirn + 2, :]
                sp_ref[g, dirn, pl.ds(rows, nb), :] = s
                new.append(a1 * s + a2 * tw + z_ref[g, 2 * dirn, pl.ds(rows, nb), :])
                new.append(a1 * tw - a2 * s + z_ref[g, 2 * dirn + 1, pl.ds(rows, nb), :])
        return tuple(new)

    init = []
    for g in range(gb):
        for dirn in range(2):
            s0 = h0_ref[g, dirn]
            init += [s0, pltpu.roll(s0, S5_STATE, 1)]
    fin = lax.fori_loop(0, n_steps, step, tuple(init))
    for g in range(gb):
        fin_ref[g, 0] = fin[4 * g]
        fin_ref[g, 1] = fin[4 * g + 2]
        for b in range(nb):
            sp_b = jnp.concatenate([sp_ref[g, dirn, pl.ds(b, n_steps, stride=nb), :] for dirn in range(2)],
                                   axis=1).astype(BF16)
            y_ref[g, :, b * lc:(b + 1) * lc] = (
                jnp.dot(u_ref[g, :, b * lc:(b + 1) * lc], w2_ref[g, 0:lc, :], preferred_element_type=F32)
                + jnp.dot(sp_b, w2_ref[g, lc:, :], preferred_element_type=F32))


def s5_scan(u, w1, w2, coef, h0, *, gb, nb):
    g, n_steps, width = u.shape
    lc = width // nb
    cols = n_steps * nb
    ns2 = 4 * S5_STATE
    return pl.pallas_call(
        functools.partial(_s5_body, gb=gb, nb=nb, n_steps=n_steps),
        grid=(g // gb,),
        in_specs=[pl.BlockSpec((gb, n_steps, width), lambda i: (i, 0, 0)),
                  pl.BlockSpec((gb, lc, 2 * ns2), lambda i: (i, 0, 0)),
                  pl.BlockSpec((gb, lc + ns2, lc), lambda i: (i, 0, 0)),
                  pl.BlockSpec((gb, 4, 2 * S5_STATE), lambda i: (i, 0, 0)),
                  pl.BlockSpec((gb, 2, nb, 2 * S5_STATE), lambda i: (i, 0, 0, 0))],
        out_specs=[pl.BlockSpec((gb, n_steps, width), lambda i: (i, 0, 0)),
                   pl.BlockSpec((gb, 2, nb, 2 * S5_STATE), lambda i: (i, 0, 0, 0))],
        out_shape=[jax.ShapeDtypeStruct((g, n_steps, width), F32),
                   jax.ShapeDtypeStruct((g, 2, nb, 2 * S5_STATE), F32)],
        scratch_shapes=[pltpu.VMEM((gb, 4, cols, 2 * S5_STATE), F32), pltpu.VMEM((gb, 2, cols, 2 * S5_STATE), F32)],
        compiler_params=_cparams(("parallel",)), name="s5_scan")(u, w1, w2, coef, h0)


def _s5_pack_body(x_ref, sel_ref, o_ref, *, n_tok, gpt, nbb):
    rows = o_ref.shape[1]
    w = o_ref.shape[2] // nbb
    acc = None
    for s in range(n_tok):
        xs = x_ref[pl.ds(s, nbb * rows, stride=n_tok), :].astype(BF16)
        part = jnp.dot(xs, sel_ref[s], preferred_element_type=F32)
        acc = part if acc is None else acc + part
    for g in range(gpt):
        for bl in range(nbb):
            o_ref[g, :, bl * w:(bl + 1) * w] = acc[bl * rows:(bl + 1) * rows, g * w:(g + 1) * w].astype(o_ref.dtype)


def _s5_unpack_body(*refs, n_tok, gpt, nbb, aliased):
    if aliased:
        refs = refs[1:]
    y_ref, q_ref, o_ref = refs
    rows = y_ref.shape[1]
    w = y_ref.shape[2] // nbb
    ycat = jnp.concatenate(
        [jnp.concatenate([y_ref[g, :, bl * w:(bl + 1) * w] for g in range(gpt)], axis=1) for bl in range(nbb)],
        axis=0)
    y_hi = ycat.astype(BF16)
    y_lo = (ycat - y_hi.astype(F32)).astype(BF16)
    for j in range(n_tok):
        qj = q_ref[j]
        o_ref[pl.ds(j, nbb * rows, stride=n_tok), :] = (jnp.dot(y_hi, qj, preferred_element_type=F32)
                                                        + jnp.dot(y_lo, qj, preferred_element_type=F32))


def _s5_selectors(n_tok, hch):
    gpt = LANES // hch
    lane = jnp.arange(LANES)
    col = (lane // hch) * (n_tok * hch) + (lane % hch)
    sel = jax.nn.one_hot(col[None, :] + hch * jnp.arange(n_tok)[:, None], gpt * n_tok * hch, dtype=BF16)
    return sel, sel.transpose(0, 2, 1)


def s5_pack(hf, sel, *, row0, nbatch, nseq, rows_blk, seqs_blk=1):
    t, d = hf.shape
    n_tok, hch = S5_SCAN_CHUNK, S5_GROUP_CH
    gpt = LANES // hch
    chunks = nseq // n_tok
    cb = chunks // rows_blk
    blk_rows = seqs_blk * rows_blk * n_tok
    assert row0 % blk_rows == 0 and chunks % rows_blk == 0 and nbatch % seqs_blk == 0
    assert seqs_blk == 1 or cb == 1
    rb0 = row0 // blk_rows
    return pl.pallas_call(
        functools.partial(_s5_pack_body, n_tok=n_tok, gpt=gpt, nbb=seqs_blk),
        grid=(nbatch // seqs_blk, cb, d // LANES),
        in_specs=[pl.BlockSpec((blk_rows, LANES), lambda b, c, l: (rb0 + b * cb + c, l)),
                  pl.BlockSpec(sel.shape, lambda b, c, l: (0, 0, 0))],
        out_specs=pl.BlockSpec((gpt, rows_blk, seqs_blk * n_tok * hch), lambda b, c, l: (l, c, b)),
        out_shape=jax.ShapeDtypeStruct((d // hch, chunks, nbatch * n_tok * hch), BF16),
        compiler_params=_cparams(("parallel", "parallel", "parallel")), name="s5_pack")(
            hf, sel)


def s5_unpack(y4, unsel, *, t, row0, rows_blk, seqs_blk=1, prev_out=None):
    grp, chunks, width = y4.shape
    w = S5_SCAN_CHUNK * S5_GROUP_CH
    nbatch = width // w
    n_tok, hch = S5_SCAN_CHUNK, S5_GROUP_CH
    gpt = LANES // hch
    d = grp * hch
    cb = chunks // rows_blk
    blk_rows = seqs_blk * rows_blk * n_tok
    assert seqs_blk == 1 or cb == 1
    rb0 = row0 // blk_rows
    in_specs = [pl.BlockSpec((gpt, rows_blk, seqs_blk * w), lambda b, c, l: (l, c, b)),
                pl.BlockSpec(unsel.shape, lambda b, c, l: (0, 0, 0))]
    args = [y4, unsel]
    aliases = {}
    if prev_out is not None:
        in_specs = [pl.BlockSpec(memory_space=pl.ANY)] + in_specs
        args = [prev_out] + args
        aliases = {0: 0}
    return pl.pallas_call(
        functools.partial(_s5_unpack_body, n_tok=n_tok, gpt=gpt, nbb=seqs_blk, aliased=prev_out is not None),
        grid=(nbatch // seqs_blk, cb, d // LANES), in_specs=in_specs,
        out_specs=pl.BlockSpec((blk_rows, LANES), lambda b, c, l: (rb0 + b * cb + c, l)),
        out_shape=jax.ShapeDtypeStruct((t, d), F32), input_output_aliases=aliases,
        compiler_params=_cparams(("parallel", "parallel", "parallel")), name="s5_unpack")(*args)


def s5_weights(lam_re, lam_im, b_re, b_im, c_re, c_im, log_dt):
    hp = lax.Precision.HIGHEST
    L, H, P = S5_SCAN_CHUNK, S5_GROUP_CH, S5_STATE
    G = lam_re.shape[1]
    dt = jnp.exp(log_dt)[..., None]
    lr, li = lam_re, lam_im
    mag = jnp.exp(lr * dt)
    ar, ai = mag * jnp.cos(li * dt), mag * jnp.sin(li * dt)
    den = lr * lr + li * li
    cr = ((ar - 1.0) * lr + ai * li) / den
    ci = (ai * lr - (ar - 1.0) * li) / den
    bbr = cr[..., None] * b_re - ci[..., None] * b_im
    bbi = cr[..., None] * b_im + ci[..., None] * b_re
    kk = jnp.arange(L + 1, dtype=F32)[:, None, None, None]
    pmag = jnp.exp(kk * (lr * dt)[None])
    pr, pi = pmag * jnp.cos(kk * (li * dt)[None]), pmag * jnp.sin(kk * (li * dt)[None])
    car = c_re[None] * pr[:, :, :, None, :] - c_im[None] * pi[:, :, :, None, :]
    cai = c_re[None] * pi[:, :, :, None, :] + c_im[None] * pr[:, :, :, None, :]
    mker = (jnp.einsum('ldgop,dgpi->ldgoi', car[:L], bbr, precision=hp)
            - jnp.einsum('ldgop,dgpi->ldgoi', cai[:L], bbi, precision=hp))
    s_idx = jnp.arange(L)[:, None]
    j_idx = jnp.arange(L)[None, :]

    def toeplitz(m, lag, mask):
        t = m[jnp.clip(lag, 0, L - 1)] * mask[:, :, None, None, None]
        return t.transpose(2, 0, 4, 1, 3)

    t_all = (toeplitz(mker[:, 0], j_idx - s_idx, (j_idx >= s_idx).astype(F32))
             + toeplitz(mker[:, 1], s_idx - j_idx, (s_idx >= j_idx).astype(F32))).reshape(G, L * H, L * H)

    def bc(dirn, powers):
        qr_, qi_ = pr[powers, dirn], pi[powers, dirn]
        re = qr_[..., None] * bbr[dirn][None] - qi_[..., None] * bbi[dirn][None]
        im = qr_[..., None] * bbi[dirn][None] + qi_[..., None] * bbr[dirn][None]
        return jnp.concatenate([re, im], axis=2).transpose(1, 0, 3, 2).reshape(G, L * H, 2 * P)

    def cc(dirn, powers):
        wr = car[powers, dirn]
        wi = cai[powers, dirn]
        return jnp.concatenate([wr, -wi], axis=3).transpose(1, 3, 0, 2).reshape(G, 2 * P, L * H)

    ar_l = jnp.arange(L)
    def with_twin(m):
        return jnp.concatenate([m, m[..., P:], m[..., :P]], axis=2)

    w1 = jnp.concatenate([with_twin(bc(0, L - 1 - ar_l)), with_twin(bc(1, ar_l))], axis=2)
    w2 = jnp.concatenate([t_all, cc(0, ar_l + 1), cc(1, L - ar_l)], axis=1)
    coef = jnp.stack([jnp.concatenate([pr[L, 0], pr[L, 0]], -1), jnp.concatenate([-pi[L, 0], pi[L, 0]], -1),
                      jnp.concatenate([pr[L, 1], pr[L, 1]], -1), jnp.concatenate([-pi[L, 1], pi[L, 1]], -1)], axis=1)
    return w1.astype(BF16), w2.astype(BF16), coef


def _dwconv_body(cur_ref, prev_ref, next_ref, w_ref, b_ref, o_ref, zp_ref, *, tile, slab, seq_prompt, rows_sub):
    t = pl.program_id(0)
    row0 = t * tile
    seq_len = jnp.where(row0 < slab, seq_prompt, slab)
    pos = row0 % seq_len
    has_prev = pos != 0
    has_next = pos + tile != seq_len
    halo = CONV_HALO
    sub = 8
    rows = tile + 2 * halo
    zp_ref[0, 0:halo, :] = jnp.where(has_prev, prev_ref[...], 0.0)
    zp_ref[0, halo:halo + tile, :] = cur_ref[...]
    zp_ref[0, halo + tile:, :] = jnp.where(has_next, next_ref[...], 0.0)
    for j in range(1, sub):
        zp_ref[j, 0:rows - sub, :] = zp_ref[0, j:j + rows - sub, :]
    w = w_ref[...]
    off = halo - CONV_WIDTH // 2
    for r0 in range(0, tile, rows_sub):
        acc = jnp.zeros((rows_sub, w.shape[1]), F32) + b_ref[...]
        for kk in range(CONV_WIDTH):
            q, j = divmod(off + kk, sub)
            acc = acc + w[kk:kk + 1, :] * zp_ref[j, r0 + sub * q:r0 + sub * q + rows_sub, :]
        o_ref[r0:r0 + rows_sub, :] = acc


def dwconv(z, w, b, *, tile, cb, slab, seq_prompt, rows_sub=32):
    m, d = z.shape
    halo = CONV_HALO
    hb = tile // halo
    n_halo_blocks = m // halo
    assert seq_prompt % tile == 0 and slab % tile == 0 and tile % halo == 0
    return pl.pallas_call(
        functools.partial(_dwconv_body, tile=tile, slab=slab, seq_prompt=seq_prompt, rows_sub=rows_sub),
        grid=(m // tile, d // cb),
        in_specs=[pl.BlockSpec((tile, cb), lambda t, c: (t, c)),
                  pl.BlockSpec((halo, cb), lambda t, c: (jnp.maximum(t * hb - 1, 0), c)),
                  pl.BlockSpec((halo, cb), lambda t, c: (jnp.minimum((t + 1) * hb, n_halo_blocks - 1), c)),
                  pl.BlockSpec((CONV_WIDTH, cb), lambda t, c: (0, c)),
                  pl.BlockSpec((1, cb), lambda t, c: (0, c))],
        out_specs=pl.BlockSpec((tile, cb), lambda t, c: (t, c)),
        out_shape=jax.ShapeDtypeStruct((m, d), F32),
        scratch_shapes=[pltpu.VMEM((8, tile + 2 * halo, cb), F32)],
        compiler_params=_cparams(("parallel", "parallel")), name="dwconv")(z, z, z, w, b)


def _moe_body(be_ref, nu_ref, x_ref, w1_ref, w3_ref, w2_ref, o_ref, w1s, w3s, w2s):
    b = pl.program_id(0)
    prev = be_ref[jnp.maximum(b - 1, 0)]

    @pl.when((b == 0) | (be_ref[b] != prev))
    def _():
        w1s[...] = w1_ref[0].astype(BF16)
        w3s[...] = w3_ref[0].astype(BF16)
        w2s[...] = w2_ref[0].astype(BF16)

    @pl.when(b < nu_ref[0])
    def _():
        x = x_ref[...]
        a = jnp.dot(x, w1s[...], preferred_element_type=F32)
        g = jnp.dot(x, w3s[...], preferred_element_type=F32)
        o_ref[...] = jnp.dot((_silu(a) * g).astype(BF16), w2s[...],
                             preferred_element_type=F32).astype(o_ref.dtype)

    @pl.when(b >= nu_ref[0])
    def _():
        o_ref[...] = jnp.zeros(o_ref.shape, o_ref.dtype)


def moe_experts(xg, block_expert, n_used, w1, w3, w2, *, tm):
    mp, d = xg.shape
    e, _, f = w1.shape
    grid_spec = pltpu.PrefetchScalarGridSpec(
        num_scalar_prefetch=2, grid=(mp // tm,),
        in_specs=[pl.BlockSpec((tm, d), lambda b, be, nu: (b, 0)),
                  pl.BlockSpec((1, d, f), lambda b, be, nu: (be[b], 0, 0)),
                  pl.BlockSpec((1, d, f), lambda b, be, nu: (be[b], 0, 0)),
                  pl.BlockSpec((1, f, d), lambda b, be, nu: (be[b], 0, 0))],
        out_specs=pl.BlockSpec((tm, d), lambda b, be, nu: (b, 0)),
        scratch_shapes=[pltpu.VMEM((d, f), BF16), pltpu.VMEM((d, f), BF16), pltpu.VMEM((f, d), BF16)])
    return pl.pallas_call(
        _moe_body, grid_spec=grid_spec, out_shape=jax.ShapeDtypeStruct((mp, d), BF16),
        compiler_params=_cparams(("arbitrary",)), name="moe_experts")(block_expert, n_used, xg, w1, w3, w2)


def _combine_body(x_ref, y1_ref, y2_ref, g_ref, gm_ref, o_ref):
    g = g_ref[...]
    o_ref[...] = x_ref[...] + gm_ref[0] * (g[:, 0:1] * y1_ref[...].astype(F32)
                                           + g[:, 1:2] * y2_ref[...].astype(F32))


def moe_combine(x, y1, y2, gates, gate_mod, *, tm, slab):
    m, d = x.shape
    tiles_per_slab = slab // tm
    row = pl.BlockSpec((tm, d), lambda i: (i, 0))
    return pl.pallas_call(
        _combine_body, grid=(m // tm,),
        in_specs=[row, row, row, pl.BlockSpec((tm, LANES), lambda i: (i, 0)),
                  pl.BlockSpec((1, 1, d), lambda i: (i // tiles_per_slab, 0, 0))],
        out_specs=row, out_shape=jax.ShapeDtypeStruct((m, d), F32),
        compiler_params=_cparams(("parallel",)), name="moe_combine")(x, y1, y2, gates, gate_mod)


def moe_layer(x, norm_g, shift, scale, gate_mod, router_w, b_router, w1, w3, w2, *, tm_rows, tm_moe, slab):
    t, d = x.shape
    n_exp = w1.shape[0]
    hb, idx, gate8, cnt = route(x, norm_g, shift, scale, router_w[0], router_w[1], b_router, tm=tm_rows, slab=slab)
    expert = idx[0:TOP_K].T
    rank = idx[TOP_K:2 * TOP_K].T
    gate = gate8[0:TOP_K].T
    counts = cnt[:, 0].astype(jnp.int32)
    padded = (counts + tm_moe - 1) // tm_moe * tm_moe
    pends = jnp.cumsum(padded)
    pstarts = pends - padded
    dest = (pstarts[expert] + rank).astype(jnp.int32).reshape(-1)
    n_assign = t * TOP_K
    n_blocks = (n_assign + n_exp * (tm_moe - 1) + tm_moe - 1) // tm_moe
    block_row0 = jnp.arange(n_blocks, dtype=jnp.int32) * tm_moe
    block_expert = jnp.minimum(jnp.sum((pends[None, :] <= block_row0[:, None]).astype(jnp.int32), axis=1),
                               n_exp - 1).astype(jnp.int32)
    n_used = (pends[-1] // tm_moe).astype(jnp.int32).reshape(1)
    tok = jnp.arange(n_assign, dtype=jnp.int32) // TOP_K
    src = jnp.zeros((n_blocks * tm_moe,), jnp.int32).at[dest].set(tok)
    xg = hb.at[src].get(mode="promise_in_bounds")
    yg = moe_experts(xg, block_expert, n_used, w1, w3, w2, tm=tm_moe)
    dest2 = dest.reshape(t, TOP_K)
    y1 = yg.at[dest2[:, 0]].get(mode="promise_in_bounds")
    y2 = yg.at[dest2[:, 1]].get(mode="promise_in_bounds")
    gates = jnp.pad(gate, ((0, 0), (0, LANES - TOP_K)))
    return moe_combine(x, y1, y2, gates, gate_mod, tm=tm_rows, slab=slab)


def _rope_tables(n_tokens):
    rows = n_tokens // GRID_W
    r = jnp.repeat(jnp.arange(rows), GRID_W).astype(F32)
    col = jnp.tile(jnp.arange(GRID_W), rows).astype(F32)
    n_freq = 64 // 4
    inv = ROPE_BASE ** (-jnp.arange(n_freq, dtype=F32) / n_freq)
    ang = jnp.concatenate([r[:, None] * inv, col[:, None] * inv], axis=-1)
    c, s = jnp.cos(ang), jnp.sin(ang)
    return jnp.concatenate([c, c, c, c], axis=1), jnp.concatenate([-s, s, -s, s], axis=1)


def _pick_tile(n, pref):
    t = pref
    while n % t:
        t //= 2
    return t


def kernel(x_prompt, x_sample, cache_mla_ckv, cache_mla_krope, cache_gqa_k, cache_gqa_v, state_s5, c, c_ctx, norm_mix_g, norm_ffn_g, ada_w, ada_b, final_norm_g, mla_wq_a, mla_q_norm, mla_wq_b, mla_wkv_a, mla_kv_norm, mla_wkv_b, mla_wo, gqa_wq, gqa_wk, gqa_wv, gqa_wo, gqa_sink, s5_lam_re, s5_lam_im, s5_b_re, s5_b_im, s5_c_re, s5_c_im, s5_log_dt, s5_d, s5_w_glu, s5_b_glu, conv_w_pw1, conv_b_pw1, conv_w_dw, conv_b_dw, conv_ln_g, conv_ln_b, conv_w_pw2, conv_b_pw2, moe_w_router, moe_b_router, moe_w1, moe_w3, moe_w2):
    bp, sp, d = x_prompt.shape
    bs, ns, _ = x_sample.shape
    depth = ada_w.shape[0]
    past = cache_mla_ckv.shape[2]
    slab = ns
    assert bp * sp == slab, "prompt rows must fill exactly one slab"
    n_slab = 1 + bs
    t = n_slab * slab
    tm = _pick_tile(slab, 1024)
    tmh = _pick_tile(slab, 512)
    tn = _pick_tile(d, 1024)
    tmq = _pick_tile(slab, 256)
    x = jnp.concatenate([x_prompt.reshape(slab, d), x_sample.reshape(bs * slab, d)], axis=0)

    rows_pad = -(-n_slab // 16) * 16
    cond = jnp.concatenate([c_ctx[None], c, jnp.zeros((rows_pad - n_slab, d), F32)], axis=0)
    mod = modulation_all(cond, ada_w, ada_b, _pick_tile(6 * d, 1024))
    mod = mod[:, :n_slab].reshape(depth, n_slab, 6, 1, d).transpose(0, 2, 1, 3, 4)

    cos_t, sin_t = _rope_tables(ns)
    w_r = moe_w_router.T
    w_r_hi = w_r.astype(BF16)
    w_r_lo = (w_r - w_r_hi.astype(F32)).astype(BF16)
    n_mixers = 4
    outs = {}

    for layer in range(depth):
        kind, r = layer % n_mixers, layer // n_mixers
        sh_mix, sc_mix, gt_mix, sh_ffn, sc_ffn, gt_ffn = [mod[layer, q] for q in range(6)]
        g_mix = norm_mix_g[layer][None]
        if kind == 0:
            qrank = mla_wq_a.shape[2]
            kvrank = mla_kv_norm.shape[1]
            hds = MLA_HEADS
            wa = jnp.concatenate([mla_wq_a[r], mla_wkv_a[r],
                                  jnp.zeros((d, LANES - MLA_ROPE), F32)], axis=1).astype(BF16)
            na = wa.shape[1]
            q_norm, kv_norm = mla_q_norm[r][None], mla_kv_norm[r][None]

            def epi_a(accs, i, j, ec, em, er, et, qrank=qrank, kvrank=kvrank, tiles=slab // tmh):
                a = accs[0]
                qn = _rms(a[:, :qrank], ec[0])
                ckv = _rms(a[:, qrank:qrank + kvrank], ec[1])
                kr = a[:, qrank + kvrank:]
                kr = jnp.where(i >= tiles, _rope_rot(kr, et[0], et[1]), kr)
                return qn, ckv, kr

            qn, ckv, kr = fused_mm(
                name="mla_proj_a", m=t, k=d, n=na, tm=tmh, tn=na, slab=slab,
                xs=[(x, 0)], pro_consts=[g_mix], pro_mods=[sh_mix, sc_mix], prologue=_pro_adaln,
                ws=[(wa, 0)], epi_tabs=[cos_t, sin_t], epi_consts=[q_norm, kv_norm], epilogue=epi_a,
                outs=[(qrank, BF16, qrank, lambda j: 0), (kvrank, F32, kvrank, lambda j: 0),
                      (LANES, F32, LANES, lambda j: 0)])
            outs['ckv'] = ckv[:slab].reshape(bp, 1, sp, kvrank)
            outs['krope'] = kr[:slab, :MLA_ROPE].reshape(bp, 1, sp, MLA_ROPE)
            wqb = mla_wq_b[r].reshape(qrank, hds, MLA_NOPE + MLA_ROPE)
            wqb_n = wqb[:, :, :MLA_NOPE].reshape(qrank, hds * MLA_NOPE).astype(BF16)
            wqb_r = wqb[:, :, MLA_NOPE:].reshape(qrank, hds * MLA_ROPE).astype(BF16)
            qscale = (MLA_NOPE + MLA_ROPE) ** -0.5 * math.log2(math.e)

            def epi_qn(accs, i, j, ec, em, er, et, qscale=qscale):
                return (accs[0] * qscale,)

            (q_nope,) = fused_mm(
                name="mla_q_nope", m=t, k=qrank, n=hds * MLA_NOPE, tm=tm, tn=_pick_tile(hds * MLA_NOPE, 1024),
                slab=slab, xs=[(qn, 0)], prologue=_pro_cast, ws=[(wqb_n, 0)], epilogue=epi_qn,
                outs=[(hds * MLA_NOPE, BF16, _pick_tile(hds * MLA_NOPE, 1024), lambda j: j)])

            def epi_qr(accs, i, j, ec, em, er, et, tiles=slab // tm, qscale=qscale):
                a = accs[0]
                return (jnp.where(i >= tiles, _rope_rot(a, et[0], et[1]), a) * qscale,)

            (q_rope,) = fused_mm(
                name="mla_q_rope", m=t, k=qrank, n=hds * MLA_ROPE, tm=tm, tn=hds * MLA_ROPE,
                slab=slab, xs=[(qn, 0)], prologue=_pro_cast, ws=[(wqb_r, 0)], epilogue=epi_qr,
                epi_tabs=[cos_t, sin_t], outs=[(hds * MLA_ROPE, BF16, hds * MLA_ROPE, lambda j: 0)])
            ckv_s = ckv[slab:].reshape(bs, ns, kvrank)
            kr_s = kr[slab:, :MLA_ROPE].reshape(bs, ns, MLA_ROPE)
            ckv_all = jnp.concatenate(
                [jnp.concatenate([ckv_s, cache_mla_ckv[:, r]], axis=1).reshape(-1, kvrank), ckv[:slab]],
                axis=0).astype(BF16)
            kr_all = jnp.concatenate(
                [jnp.concatenate([kr_s, cache_mla_krope[:, r]], axis=1).reshape(-1, MLA_ROPE),
                 kr[:slab, :MLA_ROPE]], axis=0).astype(BF16)
            wkvb = mla_wkv_b[r].reshape(kvrank, hds, MLA_NOPE + MLA_V)
            wkvb_k = wkvb[:, :, :MLA_NOPE].reshape(kvrank, hds * MLA_NOPE).astype(BF16)
            wkvb_vt = wkvb[:, :, MLA_NOPE:].reshape(kvrank, hds * MLA_V).T.astype(BF16)
            tk_rows = ckv_all.shape[0]
            tm_kv = _pick_tile(tk_rows, 1024)
            (k_nope,) = fused_mm(
                name="mla_k_expand", m=tk_rows, k=kvrank, n=hds * MLA_NOPE, tm=tm_kv,
                tn=_pick_tile(hds * MLA_NOPE, 1024), slab=tm_kv, xs=[(ckv_all, 0)], prologue=_pro_cast,
                ws=[(wkvb_k, 0)], epilogue=_epi_plain,
                outs=[(hds * MLA_NOPE, BF16, _pick_tile(hds * MLA_NOPE, 1024), lambda j: j)])
            v_t = mm_nt(wkvb_vt, ckv_all, tm=_pick_tile(tk_rows, 512), name="mla_v_expand_t")
            k_prompt0 = bs * (ns + past)
            att = mla_attention(q_nope, q_rope, k_nope, v_t, kr_all, n_heads=hds, batch=bp, nq_rows=sp,
                                nk_rows=sp, q_row0=0, k_row0=k_prompt0, tq=sp, tk=sp)
            tqs = _pick_tile(ns, 512)
            tks = next(c for c in (1536, 1152, 1024, 768, 512, 384, 256, 128) if (ns + past) % c == 0)
            att = mla_attention(q_nope, q_rope, k_nope, v_t, kr_all, n_heads=hds, batch=bs, nq_rows=ns,
                                nk_rows=ns + past, q_row0=slab, k_row0=0, tq=tqs, tk=tks, prev_out=att)
            (x,) = fused_mm(
                name="mla_out_proj", m=t, k=hds * MLA_V, n=d, tm=tmh, tn=tn, slab=slab,
                xs=[(att, 0)], prologue=_pro_cast, ws=[(mla_wo[r].astype(BF16), 0)],
                epi_mods=[gt_mix], epi_rows=[x], epilogue=_epi_residual,
                outs=[(d, F32, tn, lambda j: j)])
        elif kind == 1:
            hq, hkv, hd = GQA_HEADS, GQA_KV_HEADS, GQA_HEAD_DIM
            wqkv = jnp.concatenate([gqa_wq[r], gqa_wk[r], gqa_wv[r]], axis=1).astype(BF16)
            nqkv = wqkv.shape[1]
            tn_qkv = hkv * hd
            n_rope_tiles = (hq * hd + hkv * hd) // tn_qkv
            n_q_tiles = (hq * hd) // tn_qkv

            qscale = hd ** -0.5 * math.log2(math.e)

            def epi_qkv(accs, i, j, ec, em, er, et, tiles=slab // tmh, n_rope_tiles=n_rope_tiles,
                        n_q_tiles=n_q_tiles, qscale=qscale):
                a = accs[0]
                roped = jnp.where((i >= tiles) & (j < n_rope_tiles), _rope_rot(a, et[0], et[1]), a)
                return roped * jnp.where(j < n_q_tiles, qscale, 1.0), a

            qkv, kv_f32 = fused_mm(
                name="gqa_qkv_proj", m=t, k=d, n=nqkv, tm=tmh, tn=tn_qkv, slab=slab,
                xs=[(x, 0)], pro_consts=[g_mix], pro_mods=[sh_mix, sc_mix], prologue=_pro_adaln,
                ws=[(wqkv, 0)], epi_tabs=[cos_t, sin_t], epilogue=epi_qkv,
                outs=[(nqkv, BF16, tn_qkv, lambda j: j),
                      (2 * tn_qkv, F32, tn_qkv, functools.partial(lambda j, nq: jnp.maximum(j - nq, 0), nq=n_q_tiles))])
            outs['gqa_k'] = kv_f32[:slab, :hkv * hd].reshape(bp, 1, sp, hkv, hd)
            outs['gqa_v'] = kv_f32[:slab, hkv * hd:].reshape(bp, 1, sp, hkv, hd)
            att = gqa_attention(qkv, gqa_sink[r], batch=bp, n_rows=sp, q_row0=0, window=False)
            ctx_k = cache_gqa_k[:, r].reshape(bs, past, hkv * hd).astype(BF16)
            ctx_v = cache_gqa_v[:, r].reshape(bs, past, hkv * hd).astype(BF16)
            att = gqa_attention(qkv, gqa_sink[r], batch=bs, n_rows=ns, q_row0=slab, window=True,
                                ctx_k=ctx_k, ctx_v=ctx_v, prev_out=att)
            (x,) = fused_mm(
                name="gqa_out_proj", m=t, k=hq * hd, n=d, tm=tmh, tn=tn, slab=slab,
                xs=[(att, 0)], prologue=_pro_cast, ws=[(gqa_wo[r].astype(BF16), 0)],
                epi_mods=[gt_mix], epi_rows=[x], epilogue=_epi_residual,
                outs=[(d, F32, tn, lambda j: j)])
        elif kind == 2:
            grp, hch, pst, lch = d // S5_GROUP_CH, S5_GROUP_CH, S5_STATE, S5_SCAN_CHUNK
            hf = norm_rows(x, g_mix, tm=tmh, slab=slab, out_dtype=F32, shift=sh_mix, scale=sc_mix, name="adaln_s5")
            w1, w2, coef = s5_weights(s5_lam_re[r], s5_lam_im[r], s5_b_re[r], s5_b_im[r],
                                      s5_c_re[r], s5_c_im[r], s5_log_dt[r])

            sel, unsel = _s5_selectors(lch, hch)
            cp, cs = sp // lch, ns // lch
            rs = _pick_tile(cs, 128)
            sb = math.gcd(bp, 8)
            u_p = s5_pack(hf, sel, row0=0, nbatch=bp, nseq=sp, rows_blk=cp, seqs_blk=sb)
            u_s = s5_pack(hf, sel, row0=slab, nbatch=bs, nseq=ns, rows_blk=rs)
            h0_p = jnp.zeros((grp, 2, bp, 2 * pst), F32)
            h0_s = state_s5[:, r].transpose(2, 1, 0, 4, 3).reshape(grp, 2, bs, 2 * pst)
            gb = 2 if grp % 2 == 0 else 1
            y_p, fin_p = s5_scan(u_p, w1, w2, coef, h0_p, gb=gb, nb=bp)
            y_s, _ = s5_scan(u_s, w1, w2, coef, h0_s, gb=gb, nb=bs)
            outs['s5'] = fin_p.reshape(grp, 2, bp, 2, pst).transpose(2, 1, 0, 4, 3)[:, None]
            y = s5_unpack(y_p, unsel, t=t, row0=0, rows_blk=cp, seqs_blk=sb)
            y = s5_unpack(y_s, unsel, t=t, row0=slab, rows_blk=rs, prev_out=y)
            wg = s5_w_glu[r].astype(BF16)
            bg = s5_b_glu[r][None]
            (x,) = fused_mm(
                name="s5_glu", m=t, k=d, n=d, tm=tmq, tn=tn, slab=slab,
                xs=[(y, 0), (hf, 0)], pro_consts=[s5_d[r][None]], prologue=_pro_s5_post,
                ws=[(wg, 0), (wg, d // tn)], epi_cols=[(bg, 0), (bg, d // tn)],
                epi_mods=[gt_mix], epi_rows=[x], epilogue=_epi_glu_residual,
                outs=[(d, F32, tn, lambda j: j)])
        else:
            w1c = conv_w_pw1[r].astype(BF16)
            b1c = conv_b_pw1[r][None]
            (z,) = fused_mm(
                name="conv_pw1_glu", m=t, k=d, n=d, tm=tmh, tn=tn, slab=slab,
                xs=[(x, 0)], pro_consts=[g_mix], pro_mods=[sh_mix, sc_mix], prologue=_pro_adaln,
                ws=[(w1c, 0), (w1c, d // tn)], epi_cols=[(b1c, 0), (b1c, d // tn)], epilogue=_epi_glu,
                outs=[(d, F32, tn, lambda j: j)])
            zc = dwconv(z, conv_w_dw[r], conv_b_dw[r][None], tile=sp, cb=_pick_tile(d, 512), slab=slab,
                        seq_prompt=sp)
            (x,) = fused_mm(
                name="conv_pw2", m=t, k=d, n=d, tm=tmh, tn=tn, slab=slab,
                xs=[(zc, 0)], pro_consts=[conv_ln_g[r][None], conv_ln_b[r][None]], prologue=_pro_ln_silu,
                ws=[(conv_w_pw2[r].astype(BF16), 0)], epi_cols=[(conv_b_pw2[r][None], 0)],
                epi_mods=[gt_mix], epi_rows=[x], epilogue=_epi_bias_residual,
                outs=[(d, F32, tn, lambda j: j)])
        x = moe_layer(x, norm_ffn_g[layer][None], sh_ffn, sc_ffn, gt_ffn, (w_r_hi, w_r_lo), moe_b_router,
                      moe_w1[layer], moe_w3[layer], moe_w2[layer], tm_rows=tmh, tm_moe=256, slab=slab)

    y = norm_rows(x, final_norm_g[None], tm=tmh, slab=slab, out_dtype=F32, name="final_norm")
    return (y[:slab].reshape(bp, sp, d), y[slab:].reshape(bs, ns, d),
            outs['ckv'], outs['krope'], outs['gqa_k'], outs['gqa_v'], outs['s5'])
```

```python
import functools
import math

import jax
import jax.numpy as jnp
from jax import lax
from jax.experimental import pallas as pl
from jax.experimental.pallas import tpu as pltpu

GRID_W = 64
NORM_EPS = 1e-6
ROPE_BASE = 10000.0
NEG_INF = -1e30

MLA_HEADS = 16
MLA_NOPE = 128
MLA_ROPE = 64
MLA_V = 128

GQA_HEADS = 32
GQA_KV_HEADS = 8
GQA_HEAD_DIM = 64
WINDOW = 128
BAND_BLOCK = 128

S5_GROUP_CH = 16
S5_STATE = 64
S5_SCAN_CHUNK = 16

CONV_WIDTH = 31
CONV_HALO = 16

N_EXPERT_GROUPS = 4
TOP_K = 2

LANES = 128
VMEM_LIMIT_BYTES = 56 * 1024 * 1024

BF16 = jnp.bfloat16
F32 = jnp.float32


def _cparams(sem):
    return pltpu.CompilerParams(dimension_semantics=sem, vmem_limit_bytes=VMEM_LIMIT_BYTES)


def _silu(x):
    return x * jax.nn.sigmoid(x)


def _gelu_tanh(x):
    return 0.5 * x * (1.0 + jnp.tanh(math.sqrt(2.0 / math.pi) * (x + 0.044715 * (x * x * x))))


def _rms(x, g):
    return (x * lax.rsqrt(jnp.mean(x * x, axis=-1, keepdims=True) + NORM_EPS)) * g


def _rope_rot(x, cos_t, sin_t):
    w = x.shape[-1]
    reps = w // LANES
    c = jnp.concatenate([cos_t] * reps, axis=1) if reps > 1 else cos_t
    s = jnp.concatenate([sin_t] * reps, axis=1) if reps > 1 else sin_t
    lane = lax.broadcasted_iota(jnp.int32, x.shape, 1)
    first_half = (lane % 64) < 32
    swapped = jnp.where(first_half, pltpu.roll(x, w - 32, 1), pltpu.roll(x, 32, 1))
    return x * c + swapped * s


def _fused_mm_body(*refs, n_x, n_pc, n_pm, n_w, n_ec, n_ek, n_em, n_er, n_et, n_out, prologue, epilogue):
    pos = 0

    def take(n):
        nonlocal pos
        r = refs[pos:pos + n]
        pos += n
        return r

    x_refs = take(n_x)
    pc_refs = take(n_pc)
    pm_refs = take(n_pm)
    w_refs = take(n_w)
    ec_refs = take(n_ec + n_ek)
    em_refs = take(n_em)
    er_refs = take(n_er)
    et_refs = take(n_et)
    out_refs = take(n_out)
    (h_ref,) = take(1)
    i = pl.program_id(0)
    j = pl.program_id(1)

    @pl.when(j == 0)
    def _():
        h = prologue([r[...] for r in x_refs], [r[...] for r in pc_refs], [r[0] for r in pm_refs])
        h_ref[...] = h.astype(h_ref.dtype)

    h = h_ref[...]
    accs = [jnp.dot(h, w_ref[...].astype(BF16), preferred_element_type=F32) for w_ref in w_refs]
    outs = epilogue(accs, i, j, [r[...] for r in ec_refs], [r[0] for r in em_refs],
                    [r[...] for r in er_refs], [r[...] for r in et_refs])
    for o_ref, o in zip(out_refs, outs):
        o_ref[...] = o.astype(o_ref.dtype)


def fused_mm(*, name, m, k, n, tm, tn, slab, xs, prologue, ws, epilogue, outs,
             pro_consts=(), pro_mods=(), epi_cols=(), epi_consts=(), epi_mods=(), epi_rows=(), epi_tabs=()):
    assert m % tm == 0 and n % tn == 0 and slab % tm == 0
    grid = (m // tm, n // tn)
    tiles_per_slab = slab // tm
    in_specs, args = [], []
    for arr, off in xs:
        in_specs.append(pl.BlockSpec((tm, k), functools.partial(lambda i, j, off: (i + off, 0), off=off)))
        args.append(arr)
    for arr in pro_consts:
        in_specs.append(pl.BlockSpec((1, k), lambda i, j: (0, 0)))
        args.append(arr)
    for arr in pro_mods:
        in_specs.append(pl.BlockSpec((1, 1, k), lambda i, j: (i // tiles_per_slab, 0, 0)))
        args.append(arr)
    for arr, off in ws:
        in_specs.append(pl.BlockSpec((k, tn), functools.partial(lambda i, j, off: (0, j + off), off=off)))
        args.append(arr)
    for arr, off in epi_cols:
        in_specs.append(pl.BlockSpec((1, tn), functools.partial(lambda i, j, off: (0, j + off), off=off)))
        args.append(arr)
    for arr in epi_consts:
        in_specs.append(pl.BlockSpec(arr.shape, lambda i, j: (0, 0)))
        args.append(arr)
    for arr in epi_mods:
        in_specs.append(pl.BlockSpec((1, 1, tn), lambda i, j: (i // tiles_per_slab, 0, j)))
        args.append(arr)
    for arr in epi_rows:
        in_specs.append(pl.BlockSpec((tm, tn), lambda i, j: (i, j)))
        args.append(arr)
    for arr in epi_tabs:
        in_specs.append(pl.BlockSpec((tm, arr.shape[1]), lambda i, j: (i % tiles_per_slab, 0)))
        args.append(arr)
    out_specs, out_shapes = [], []
    for cols, dtype, bcols, cfn in outs:
        out_specs.append(pl.BlockSpec((tm, bcols), functools.partial(lambda i, j, cfn: (i, cfn(j)), cfn=cfn)))
        out_shapes.append(jax.ShapeDtypeStruct((m, cols), dtype))
    body = functools.partial(
        _fused_mm_body, n_x=len(xs), n_pc=len(pro_consts), n_pm=len(pro_mods), n_w=len(ws),
        n_ec=len(epi_cols), n_ek=len(epi_consts), n_em=len(epi_mods), n_er=len(epi_rows), n_et=len(epi_tabs),
        n_out=len(outs), prologue=prologue, epilogue=epilogue)
    res = pl.pallas_call(
        body, grid=grid, in_specs=in_specs, out_specs=out_specs, out_shape=out_shapes,
        scratch_shapes=[pltpu.VMEM((tm, k), BF16)],
        compiler_params=_cparams(("parallel", "arbitrary")), name=name)(*args)
    return res


def _pro_cast(xv, cv, mv):
    return xv[0]


def _pro_adaln(xv, cv, mv):
    return _rms(xv[0], cv[0]) * (1.0 + mv[1]) + mv[0]


def _pro_rms(xv, cv, mv):
    return _rms(xv[0], cv[0])


def _pro_ln_silu(xv, cv, mv):
    x = xv[0]
    mu = jnp.mean(x, axis=-1, keepdims=True)
    xc = x - mu
    var = jnp.mean(xc * xc, axis=-1, keepdims=True)
    return _silu((xc * lax.rsqrt(var + NORM_EPS)) * cv[0] + cv[1])


def _pro_s5_post(xv, cv, mv):
    return _gelu_tanh(xv[0] + cv[0] * xv[1])


def _epi_plain(accs, i, j, ec, em, er, et):
    return (accs[0],)


def _epi_bias_residual(accs, i, j, ec, em, er, et):
    return (er[0] + em[0] * (accs[0] + ec[0]),)


def _epi_residual(accs, i, j, ec, em, er, et):
    return (er[0] + em[0] * accs[0],)


def _epi_glu(accs, i, j, ec, em, er, et):
    return ((accs[0] + ec[0]) * jax.nn.sigmoid(accs[1] + ec[1]),)


def _epi_glu_residual(accs, i, j, ec, em, er, et):
    return (er[0] + em[0] * ((accs[0] + ec[0]) * jax.nn.sigmoid(accs[1] + ec[1])),)


def _modulation_body(c_ref, w_ref, b_ref, o_ref):
    h = _silu(c_ref[...]).astype(BF16)
    o_ref[0] = jnp.dot(h, w_ref[0].astype(BF16), preferred_element_type=F32) + b_ref[0]


def modulation_all(cond, ada_w, ada_b, tn):
    depth, d, n = ada_w.shape
    rows = cond.shape[0]
    return pl.pallas_call(
        _modulation_body, grid=(depth, n // tn),
        in_specs=[pl.BlockSpec((rows, d), lambda l, j: (0, 0)),
                  pl.BlockSpec((1, d, tn), lambda l, j: (l, 0, j)),
                  pl.BlockSpec((1, 1, tn), lambda l, j: (l, 0, j))],
        out_specs=pl.BlockSpec((1, rows, tn), lambda l, j: (l, 0, j)),
        out_shape=jax.ShapeDtypeStruct((depth, rows, n), F32),
        compiler_params=_cparams(("parallel", "parallel")), name="modulation")(
            cond, ada_w, ada_b.reshape(depth, 1, n))


def _norm_body(*refs, modulated):
    x_ref, g_ref = refs[0], refs[1]
    y = _rms(x_ref[...], g_ref[...])
    if modulated:
        shift_ref, scale_ref = refs[2], refs[3]
        y = y * (1.0 + scale_ref[0]) + shift_ref[0]
    o_ref = refs[-1]
    o_ref[...] = y.astype(o_ref.dtype)


def norm_rows(x, g, *, tm, slab, out_dtype, shift=None, scale=None, name="norm"):
    m, d = x.shape
    tiles_per_slab = slab // tm
    in_specs = [pl.BlockSpec((tm, d), lambda i: (i, 0)), pl.BlockSpec((1, d), lambda i: (0, 0))]
    args = [x, g]
    modulated = shift is not None
    if modulated:
        in_specs += [pl.BlockSpec((1, 1, d), lambda i: (i // tiles_per_slab, 0, 0))] * 2
        args += [shift, scale]
    return pl.pallas_call(
        functools.partial(_norm_body, modulated=modulated),
        grid=(m // tm,), in_specs=in_specs, out_specs=pl.BlockSpec((tm, d), lambda i: (i, 0)),
        out_shape=jax.ShapeDtypeStruct((m, d), out_dtype),
        compiler_params=_cparams(("parallel",)), name=name)(*args)


def _route_body(x_ref, g_ref, shift_ref, scale_ref, whi_ref, wlo_ref, b_ref, tri_ref,
                h_ref, idx_ref, gate_ref, cnt_ref, carry_ref, *, n_groups):
    @pl.when(pl.program_id(0) == 0)
    def _():
        carry_ref[...] = jnp.zeros(carry_ref.shape, F32)

    y = _rms(x_ref[...], g_ref[...]) * (1.0 + scale_ref[0]) + shift_ref[0]
    y_hi = y.astype(BF16)
    h_ref[...] = y_hi
    y_lo = (y - y_hi.astype(F32)).astype(BF16)
    nt = (((1,), (1,)), ((), ()))
    w_hi = whi_ref[...]
    logits = (lax.dot_general(w_hi, y_hi, nt, preferred_element_type=F32)
              + lax.dot_general(w_hi, y_lo, nt, preferred_element_type=F32)
              + lax.dot_general(wlo_ref[...], y_hi, nt, preferred_element_type=F32))
    n_exp, tm = logits.shape
    epg = n_exp // n_groups
    scores = jax.nn.sigmoid(logits)
    biased = scores + b_ref[...]
    sub = lax.broadcasted_iota(jnp.int32, (epg, tm), 0).astype(F32)
    best = sel = loc1 = loc2 = None
    for grp in range(n_groups):
        blk = biased[grp * epg:(grp + 1) * epg]
        m1 = jnp.max(blk, axis=0, keepdims=True)
        i1 = jnp.min(jnp.where(blk == m1, sub, float(epg)), axis=0, keepdims=True)
        rest = jnp.where(sub == i1, -jnp.inf, blk)
        m2 = jnp.max(rest, axis=0, keepdims=True)
        i2 = jnp.min(jnp.where(rest == m2, sub, float(epg)), axis=0, keepdims=True)
        gsum = m1 + m2
        if grp == 0:
            best, sel, loc1, loc2 = gsum, jnp.zeros_like(gsum), i1, i2
        else:
            better = gsum > best
            best = jnp.where(better, gsum, best)
            sel = jnp.where(better, float(grp), sel)
            loc1 = jnp.where(better, i1, loc1)
            loc2 = jnp.where(better, i2, loc2)
    e1 = sel * epg + loc1
    e2 = sel * epg + loc2
    row = lax.broadcasted_iota(jnp.int32, (n_exp, tm), 0).astype(F32)
    hit1, hit2 = row == e1, row == e2
    s1 = jnp.sum(jnp.where(hit1, scores, 0.0), axis=0, keepdims=True)
    s2 = jnp.sum(jnp.where(hit2, scores, 0.0), axis=0, keepdims=True)
    total = s1 + s2
    onehot = jnp.where(hit1, 1.0, 0.0) + jnp.where(hit2, 1.0, 0.0)
    csum = jnp.dot(onehot.astype(BF16), tri_ref[...], preferred_element_type=F32)
    before = carry_ref[...] + csum - onehot
    r1 = jnp.sum(jnp.where(hit1, before, 0.0), axis=0, keepdims=True)
    r2 = jnp.sum(jnp.where(hit2, before, 0.0), axis=0, keepdims=True)
    carry = carry_ref[...] + csum[:, tm - 1:tm]
    carry_ref[...] = carry
    cnt_ref[...] = jnp.broadcast_to(carry, cnt_ref.shape)
    idx_ref[...] = jnp.concatenate([e1, e2, r1, r2, jnp.zeros((4, tm), F32)], axis=0).astype(jnp.int32)
    gate_ref[...] = jnp.concatenate([s1 / total, s2 / total, jnp.zeros((6, tm), F32)], axis=0)


def route(x, g, shift, scale, wt_hi, wt_lo, b_router, *, tm, slab):
    m, d = x.shape
    n_exp = wt_hi.shape[0]
    tiles_per_slab = slab // tm
    tri = jnp.triu(jnp.ones((tm, tm), BF16))
    const = lambda i: (0, 0)
    return pl.pallas_call(
        functools.partial(_route_body, n_groups=N_EXPERT_GROUPS),
        grid=(m // tm,),
        in_specs=[pl.BlockSpec((tm, d), lambda i: (i, 0)), pl.BlockSpec((1, d), const),
                  pl.BlockSpec((1, 1, d), lambda i: (i // tiles_per_slab, 0, 0)),
                  pl.BlockSpec((1, 1, d), lambda i: (i // tiles_per_slab, 0, 0)),
                  pl.BlockSpec((n_exp, d), const), pl.BlockSpec((n_exp, d), const),
                  pl.BlockSpec((n_exp, 1), const), pl.BlockSpec((tm, tm), const)],
        out_specs=[pl.BlockSpec((tm, d), lambda i: (i, 0)), pl.BlockSpec((8, tm), lambda i: (0, i)),
                   pl.BlockSpec((8, tm), lambda i: (0, i)), pl.BlockSpec((n_exp, LANES), const)],
        out_shape=[jax.ShapeDtypeStruct((m, d), BF16), jax.ShapeDtypeStruct((8, m), jnp.int32),
                   jax.ShapeDtypeStruct((8, m), F32), jax.ShapeDtypeStruct((n_exp, LANES), F32)],
        scratch_shapes=[pltpu.VMEM((n_exp, 1), F32)],
        compiler_params=_cparams(("arbitrary",)), name="adaln_route")(
            x, g, shift, scale, wt_hi, wt_lo, b_router.reshape(n_exp, 1), tri)


def _mla_attn_body(*refs, nk, nh, aliased):
    if aliased:
        refs = refs[1:]
    qn_ref, qr_ref, kn_ref, vt_ref, kr_ref, o_ref, m_ref, l_ref, acc_ref = refs
    ki = pl.program_id(3)

    @pl.when(ki == 0)
    def _():
        m_ref[...] = jnp.full(m_ref.shape, -jnp.inf, F32)
        l_ref[...] = jnp.zeros(l_ref.shape, F32)
        acc_ref[...] = jnp.zeros(acc_ref.shape, F32)

    kr = kr_ref[...]
    scores, probs = {}, {}

    def qk(hh):
        q = jnp.concatenate([qn_ref[:, hh * MLA_NOPE:(hh + 1) * MLA_NOPE],
                             qr_ref[:, hh * MLA_ROPE:(hh + 1) * MLA_ROPE]], axis=1)
        k = jnp.concatenate([kn_ref[:, hh * MLA_NOPE:(hh + 1) * MLA_NOPE], kr], axis=1)
        scores[hh] = lax.dot_general(k, q, (((1,), (1,)), ((), ())), preferred_element_type=F32)

    def softmax(hh):
        st = scores.pop(hh)
        m_prev = m_ref[hh]
        m_new = jnp.maximum(m_prev, jnp.max(st, axis=0, keepdims=True))
        alpha = jnp.exp2(m_prev - m_new)
        p = jnp.exp2(st - m_new)
        l_ref[hh] = alpha * l_ref[hh] + jnp.sum(p, axis=0, keepdims=True)
        m_ref[hh] = m_new
        probs[hh] = (alpha, p.astype(BF16))

    def pv(hh):
        alpha, p = probs.pop(hh)
        acc_ref[hh] = alpha * acc_ref[hh] + jnp.dot(
            vt_ref[hh * MLA_V:(hh + 1) * MLA_V, :], p, preferred_element_type=F32)

    for t in range(nh + 2):
        if t < nh:
            qk(t)
        if 0 <= t - 1 < nh:
            softmax(t - 1)
        if 0 <= t - 2 < nh:
            pv(t - 2)

    @pl.when(ki == nk - 1)
    def _():
        o_ref[...] = jnp.concatenate([(acc_ref[hh] / l_ref[hh]).T for hh in range(nh)],
                                     axis=1).astype(o_ref.dtype)


def mla_attention(qn, qr, kn, vt, kr, *, n_heads, batch, nq_rows, nk_rows, q_row0, k_row0, tq, tk, prev_out=None):
    t = qn.shape[0]
    nh = min(4, n_heads)
    hg = n_heads // nh
    nq, nk = nq_rows // tq, nk_rows // tk
    assert q_row0 % tq == 0 and k_row0 % tk == 0 and nq_rows % tq == 0 and nk_rows % tk == 0
    qb0, kb0 = q_row0 // tq, k_row0 // tk

    def qmap(b, h, qi, ki):
        return (qb0 + b * nq + qi, h)

    def kmap(b, h, qi, ki):
        return (kb0 + b * nk + ki, h)

    in_specs = [pl.BlockSpec((tq, nh * MLA_NOPE), qmap),
                pl.BlockSpec((tq, nh * MLA_ROPE), qmap),
                pl.BlockSpec((tk, nh * MLA_NOPE), kmap),
                pl.BlockSpec((nh * MLA_V, tk), lambda b, h, qi, ki: (h, kb0 + b * nk + ki)),
                pl.BlockSpec((tk, MLA_ROPE), lambda b, h, qi, ki: (kb0 + b * nk + ki, 0))]
    args = [qn, qr, kn, vt, kr]
    aliases = {}
    if prev_out is not None:
        in_specs = [pl.BlockSpec(memory_space=pl.ANY)] + in_specs
        args = [prev_out] + args
        aliases = {0: 0}
    return pl.pallas_call(
        functools.partial(_mla_attn_body, nk=nk, nh=nh, aliased=prev_out is not None),
        grid=(batch, hg, nq, nk), in_specs=in_specs,
        out_specs=pl.BlockSpec((tq, nh * MLA_V), qmap),
        out_shape=jax.ShapeDtypeStruct((t, n_heads * MLA_V), BF16),
        scratch_shapes=[pltpu.VMEM((nh, 1, tq), F32), pltpu.VMEM((nh, 1, tq), F32),
                        pltpu.VMEM((nh, MLA_V, tq), F32)],
        input_output_aliases=aliases,
        compiler_params=_cparams(("parallel", "parallel", "parallel", "arbitrary")),
        name="mla_attention")(*args)


def _mm_nt_body(w_ref, x_ref, o_ref):
    o_ref[...] = lax.dot_general(w_ref[...], x_ref[...], (((1,), (1,)), ((), ())),
                                 preferred_element_type=F32).astype(o_ref.dtype)


def mm_nt(wt, x, *, tm, name):
    n, k = wt.shape
    m = x.shape[0]
    return pl.pallas_call(
        _mm_nt_body, grid=(m // tm,),
        in_specs=[pl.BlockSpec((n, k), lambda i: (0, 0)), pl.BlockSpec((tm, k), lambda i: (i, 0))],
        out_specs=pl.BlockSpec((n, tm), lambda i: (0, i)),
        out_shape=jax.ShapeDtypeStruct((n, m), BF16),
        compiler_params=_cparams(("parallel",)), name=name)(wt, x)


def _gqa_attn_body(*refs, n_seg, window, nq, aliased, group):
    if aliased:
        refs = refs[1:]
    sink_ref, q_ref = refs[0], refs[1]
    k_refs = refs[2:2 + n_seg]
    v_refs = refs[2 + n_seg:2 + 2 * n_seg]
    o_ref = refs[2 + 2 * n_seg]
    p_idx = pl.program_id(1)
    qi = pl.program_id(2)
    d = GQA_HEAD_DIM
    q = q_ref[...]
    tq = q.shape[0]
    ks = [r[...] for r in k_refs]
    vs = [r[...] for r in v_refs]
    if ks[-1].ndim == 3:
        ks[-1] = ks[-1][0]
        vs[-1] = vs[-1][0]
    n_loc = 3 * BAND_BLOCK
    if window:
        key = lax.broadcasted_iota(jnp.int32, (n_loc, tq), 0)
        qry = lax.broadcasted_iota(jnp.int32, (n_loc, tq), 1)
        lo = jnp.where(qi > 0, 0, BAND_BLOCK)
        hi = jnp.where(qi < nq - 1, n_loc, 2 * BAND_BLOCK)
        valid1 = (jnp.abs(qry + BAND_BLOCK - key) <= WINDOW) & (key >= lo) & (key < hi)
        valid = jnp.concatenate([valid1.astype(F32)] * group, axis=1) > 0.5
    log2e = math.log2(math.e)
    o_t = []
    for kvh in range(2):
        k = jnp.concatenate([kk[:, kvh * d:(kvh + 1) * d] for kk in ks], axis=0)
        v = jnp.concatenate([vv[:, kvh * d:(kvh + 1) * d] for vv in vs], axis=0)
        base = kvh * group * d
        q4 = jnp.concatenate([q[:, base + g * d: base + (g + 1) * d] for g in range(group)], axis=0)
        s = lax.dot_general(k, q4, (((1,), (1,)), ((), ())), preferred_element_type=F32)
        head0 = (p_idx * 2 + kvh) * group
        snk = jnp.concatenate([jnp.full((1, tq), sink_ref[head0 + g] * log2e, F32) for g in range(group)], axis=1)
        if window:
            s_loc = jnp.where(valid, s[:n_loc], NEG_INF)
            s_ctx = s[n_loc:]
            m = jnp.maximum(jnp.maximum(jnp.max(s_loc, axis=0, keepdims=True),
                                        jnp.max(s_ctx, axis=0, keepdims=True)), snk)
            p = jnp.concatenate([jnp.exp2(s_loc - m), jnp.exp2(s_ctx - m)], axis=0)
        else:
            m = jnp.maximum(jnp.max(s, axis=0, keepdims=True), snk)
            p = jnp.exp2(s - m)
        denom = jnp.sum(p, axis=0, keepdims=True) + jnp.exp2(snk - m)
        o_t.append(lax.dot_general(v, p.astype(BF16), (((0,), (0,)), ((), ())),
                                   preferred_element_type=F32) / denom)
    o = jnp.concatenate(o_t, axis=0).T
    pieces = [o[g * tq:(g + 1) * tq, kvh * d:(kvh + 1) * d] for kvh in range(2) for g in range(group)]
    o_ref[...] = jnp.concatenate(pieces, axis=1).astype(o_ref.dtype)


def gqa_attention(qkv, sink, *, batch, n_rows, q_row0, window, ctx_k=None, ctx_v=None, prev_out=None):
    t = qkv.shape[0]
    d, h, kvh = GQA_HEAD_DIM, GQA_HEADS, GQA_KV_HEADS
    group = h // kvh
    pairs = kvh // 2
    qw = 2 * group * d
    kcol0 = (h * d) // LANES
    vcol0 = (h * d + kvh * d) // LANES
    tq = BAND_BLOCK if window else n_rows
    nq = n_rows // tq
    qb0 = q_row0 // tq
    assert q_row0 % tq == 0

    def qmap(b, p, i):
        return (qb0 + b * nq + i, p)

    in_specs = [pl.BlockSpec(memory_space=pltpu.SMEM), pl.BlockSpec((tq, qw), qmap)]
    args = [sink, qkv]
    if window:
        def kv_specs(col0):
            return [
                pl.BlockSpec((tq, LANES), lambda b, p, i: (qb0 + b * nq + jnp.maximum(i - 1, 0), col0 + p)),
                pl.BlockSpec((tq, LANES), lambda b, p, i: (qb0 + b * nq + i, col0 + p)),
                pl.BlockSpec((tq, LANES), lambda b, p, i: (qb0 + b * nq + jnp.minimum(i + 1, nq - 1), col0 + p)),
            ]
        n_ctx = ctx_k.shape[1]
        ctx_spec = pl.BlockSpec((1, n_ctx, LANES), lambda b, p, i: (b, 0, p))
        in_specs += kv_specs(kcol0) + [ctx_spec] + kv_specs(vcol0) + [ctx_spec]
        args += [qkv, qkv, qkv, ctx_k, qkv, qkv, qkv, ctx_v]
        n_seg = 4
    else:
        in_specs += [pl.BlockSpec((tq, LANES), lambda b, p, i: (qb0 + b * nq + i, kcol0 + p)),
                     pl.BlockSpec((tq, LANES), lambda b, p, i: (qb0 + b * nq + i, vcol0 + p))]
        args += [qkv, qkv]
        n_seg = 1
    aliases = {}
    if prev_out is not None:
        in_specs = [pl.BlockSpec(memory_space=pl.ANY)] + in_specs
        args = [prev_out] + args
        aliases = {0: 0}
    return pl.pallas_call(
        functools.partial(_gqa_attn_body, n_seg=n_seg, window=window, nq=nq,
                          aliased=prev_out is not None, group=group),
        grid=(batch, pairs, nq), in_specs=in_specs,
        out_specs=pl.BlockSpec((tq, qw), qmap),
        out_shape=jax.ShapeDtypeStruct((t, h * d), BF16),
        input_output_aliases=aliases,
        compiler_params=_cparams(("parallel", "parallel", "parallel")),
        name="gqa_window_attention" if window else "gqa_dense_attention")(*args)


def _s5_body(u_ref, w1_ref, w2_ref, coef_ref, h0_ref, y_ref, fin_ref, z_ref, sp_ref, *, gb, nb, n_steps):
    lc = S5_SCAN_CHUNK * S5_GROUP_CH
    ns = 2 * S5_STATE
    for g in range(gb):
        for b in range(nb):
            z = jnp.dot(u_ref[g, :, b * lc:(b + 1) * lc], w1_ref[g], preferred_element_type=F32)
            for k in range(4):
                z_ref[g, k, pl.ds(b, n_steps, stride=nb), :] = z[:, k * ns:(k + 1) * ns]

    coef = coef_ref[...]

    def step(c, carry):
        new = []
        rf = pl.multiple_of(c * nb, nb)
        rb = pl.multiple_of((n_steps - 1 - c) * nb, nb)
        for g in range(gb):
            for dirn, rows in ((0, rf), (1, rb)):
                s, tw = carry[4 * g + 2 * dirn], carry[4 * g + 2 * dirn + 1]
                a1 = coef[g, 2 * dirn:2 * dirn + 1, :]
                a2 = coef[g, 2 * dirn + 1:2 * dirn + 2, :]
                sp_ref[g, dirn, pl.ds(rows, nb), :] = s
                new.append(a1 * s + a2 * tw + z_ref[g, 2 * dirn, pl.ds(rows, nb), :])
                new.append(a1 * tw - a2 * s + z_ref[g, 2 * dirn + 1, pl.ds(rows, nb), :])
        return tuple(new)

    init = []
    for g in range(gb):
        for dirn in range(2):
            s0 = h0_ref[g, dirn]
            init += [s0, pltpu.roll(s0, S5_STATE, 1)]
    fin = lax.fori_loop(0, n_steps, step, tuple(init))
    for g in range(gb):
        fin_ref[g, 0] = fin[4 * g]
        fin_ref[g, 1] = fin[4 * g + 2]
        for b in range(nb):
            sp_b = jnp.concatenate([sp_ref[g, dirn, pl.ds(b, n_steps, stride=nb), :] for dirn in range(2)],
                                   axis=1).astype(BF16)
            y_ref[g, :, b * lc:(b + 1) * lc] = (
                jnp.dot(u_ref[g, :, b * lc:(b + 1) * lc], w2_ref[g, 0:lc, :], preferred_element_type=F32)
                + jnp.dot(sp_b, w2_ref[g, lc:, :], preferred_element_type=F32))


def s5_scan(u, w1, w2, coef, h0, *, gb, nb):
    g, n_steps, width = u.shape
    lc = width // nb
    cols = n_steps * nb
    ns2 = 4 * S5_STATE
    return pl.pallas_call(
        functools.partial(_s5_body, gb=gb, nb=nb, n_steps=n_steps),
        grid=(g // gb,),
        in_specs=[pl.BlockSpec((gb, n_steps, width), lambda i: (i, 0, 0)),
                  pl.BlockSpec((gb, lc, 2 * ns2), lambda i: (i, 0, 0)),
                  pl.BlockSpec((gb, lc + ns2, lc), lambda i: (i, 0, 0)),
                  pl.BlockSpec((gb, 4, 2 * S5_STATE), lambda i: (i, 0, 0)),
                  pl.BlockSpec((gb, 2, nb, 2 * S5_STATE), lambda i: (i, 0, 0, 0))],
        out_specs=[pl.BlockSpec((gb, n_steps, width), lambda i: (i, 0, 0)),
                   pl.BlockSpec((gb, 2, nb, 2 * S5_STATE), lambda i: (i, 0, 0, 0))],
        out_shape=[jax.ShapeDtypeStruct((g, n_steps, width), F32),
                   jax.ShapeDtypeStruct((g, 2, nb, 2 * S5_STATE), F32)],
        scratch_shapes=[pltpu.VMEM((gb, 4, cols, 2 * S5_STATE), F32), pltpu.VMEM((gb, 2, cols, 2 * S5_STATE), F32)],
        compiler_params=_cparams(("parallel",)), name="s5_scan")(u, w1, w2, coef, h0)


def _s5_pack_body(x_ref, sel_ref, o_ref, *, n_tok, gpt, nbb):
    rows = o_ref.shape[1]
    w = o_ref.shape[2] // nbb
    acc = None
    for s in range(n_tok):
        xs = x_ref[pl.ds(s, nbb * rows, stride=n_tok), :].astype(BF16)
        part = jnp.dot(xs, sel_ref[s], preferred_element_type=F32)
        acc = part if acc is None else acc + part
    for g in range(gpt):
        for bl in range(nbb):
            o_ref[g, :, bl * w:(bl + 1) * w] = acc[bl * rows:(bl + 1) * rows, g * w:(g + 1) * w].astype(o_ref.dtype)


def _s5_unpack_body(*refs, n_tok, gpt, nbb, aliased):
    if aliased:
        refs = refs[1:]
    y_ref, q_ref, o_ref = refs
    rows = y_ref.shape[1]
    w = y_ref.shape[2] // nbb
    ycat = jnp.concatenate(
        [jnp.concatenate([y_ref[g, :, bl * w:(bl + 1) * w] for g in range(gpt)], axis=1) for bl in range(nbb)],
        axis=0)
    y_hi = ycat.astype(BF16)
    y_lo = (ycat - y_hi.astype(F32)).astype(BF16)
    for j in range(n_tok):
        qj = q_ref[j]
        o_ref[pl.ds(j, nbb * rows, stride=n_tok), :] = (jnp.dot(y_hi, qj, preferred_element_type=F32)
                                                        + jnp.dot(y_lo, qj, preferred_element_type=F32))


def _s5_selectors(n_tok, hch):
    gpt = LANES // hch
    lane = jnp.arange(LANES)
    col = (lane // hch) * (n_tok * hch) + (lane % hch)
    sel = jax.nn.one_hot(col[None, :] + hch * jnp.arange(n_tok)[:, None], gpt * n_tok * hch, dtype=BF16)
    return sel, sel.transpose(0, 2, 1)


def s5_pack(hf, sel, *, row0, nbatch, nseq, rows_blk, seqs_blk=1):
    t, d = hf.shape
    n_tok, hch = S5_SCAN_CHUNK, S5_GROUP_CH
    gpt = LANES // hch
    chunks = nseq // n_tok
    cb = chunks // rows_blk
    blk_rows = seqs_blk * rows_blk * n_tok
    assert row0 % blk_rows == 0 and chunks % rows_blk == 0 and nbatch % seqs_blk == 0
    assert seqs_blk == 1 or cb == 1
    rb0 = row0 // blk_rows
    return pl.pallas_call(
        functools.partial(_s5_pack_body, n_tok=n_tok, gpt=gpt, nbb=seqs_blk),
        grid=(nbatch // seqs_blk, cb, d // LANES),
        in_specs=[pl.BlockSpec((blk_rows, LANES), lambda b, c, l: (rb0 + b * cb + c, l)),
                  pl.BlockSpec(sel.shape, lambda b, c, l: (0, 0, 0))],
        out_specs=pl.BlockSpec((gpt, rows_blk, seqs_blk * n_tok * hch), lambda b, c, l: (l, c, b)),
        out_shape=jax.ShapeDtypeStruct((d // hch, chunks, nbatch * n_tok * hch), BF16),
        compiler_params=_cparams(("parallel", "parallel", "parallel")), name="s5_pack")(
            hf, sel)


def s5_unpack(y4, unsel, *, t, row0, rows_blk, seqs_blk=1, prev_out=None):
    grp, chunks, width = y4.shape
    w = S5_SCAN_CHUNK * S5_GROUP_CH
    nbatch = width // w
    n_tok, hch = S5_SCAN_CHUNK, S5_GROUP_CH
    gpt = LANES // hch
    d = grp * hch
    cb = chunks // rows_blk
    blk_rows = seqs_blk * rows_blk * n_tok
    assert seqs_blk == 1 or cb == 1
    rb0 = row0 // blk_rows
    in_specs = [pl.BlockSpec((gpt, rows_blk, seqs_blk * w), lambda b, c, l: (l, c, b)),
                pl.BlockSpec(unsel.shape, lambda b, c, l: (0, 0, 0))]
    args = [y4, unsel]
    aliases = {}
    if prev_out is not None:
        in_specs = [pl.BlockSpec(memory_space=pl.ANY)] + in_specs
        args = [prev_out] + args
        aliases = {0: 0}
    return pl.pallas_call(
        functools.partial(_s5_unpack_body, n_tok=n_tok, gpt=gpt, nbb=seqs_blk, aliased=prev_out is not None),
        grid=(nbatch // seqs_blk, cb, d // LANES), in_specs=in_specs,
        out_specs=pl.BlockSpec((blk_rows, LANES), lambda b, c, l: (rb0 + b * cb + c, l)),
        out_shape=jax.ShapeDtypeStruct((t, d), F32), input_output_aliases=aliases,
        compiler_params=_cparams(("parallel", "parallel", "parallel")), name="s5_unpack")(*args)


def s5_weights(lam_re, lam_im, b_re, b_im, c_re, c_im, log_dt):
    hp = lax.Precision.HIGHEST
    L, H, P = S5_SCAN_CHUNK, S5_GROUP_CH, S5_STATE
    G = lam_re.shape[1]
    dt = jnp.exp(log_dt)[..., None]
    lr, li = lam_re, lam_im
    mag = jnp.exp(lr * dt)
    ar, ai = mag * jnp.cos(li * dt), mag * jnp.sin(li * dt)
    den = lr * lr + li * li
    cr = ((ar - 1.0) * lr + ai * li) / den
    ci = (ai * lr - (ar - 1.0) * li) / den
    bbr = cr[..., None] * b_re - ci[..., None] * b_im
    bbi = cr[..., None] * b_im + ci[..., None] * b_re
    kk = jnp.arange(L + 1, dtype=F32)[:, None, None, None]
    pmag = jnp.exp(kk * (lr * dt)[None])
    pr, pi = pmag * jnp.cos(kk * (li * dt)[None]), pmag * jnp.sin(kk * (li * dt)[None])
    car = c_re[None] * pr[:, :, :, None, :] - c_im[None] * pi[:, :, :, None, :]
    cai = c_re[None] * pi[:, :, :, None, :] + c_im[None] * pr[:, :, :, None, :]
    mker = (jnp.einsum('ldgop,dgpi->ldgoi', car[:L], bbr, precision=hp)
            - jnp.einsum('ldgop,dgpi->ldgoi', cai[:L], bbi, precision=hp))
    s_idx = jnp.arange(L)[:, None]
    j_idx = jnp.arange(L)[None, :]

    def toeplitz(m, lag, mask):
        t = m[jnp.clip(lag, 0, L - 1)] * mask[:, :, None, None, None]
        return t.transpose(2, 0, 4, 1, 3)

    t_all = (toeplitz(mker[:, 0], j_idx - s_idx, (j_idx >= s_idx).astype(F32))
             + toeplitz(mker[:, 1], s_idx - j_idx, (s_idx >= j_idx).astype(F32))).reshape(G, L * H, L * H)

    def bc(dirn, powers):
        qr_, qi_ = pr[powers, dirn], pi[powers, dirn]
        re = qr_[..., None] * bbr[dirn][None] - qi_[..., None] * bbi[dirn][None]
        im = qr_[..., None] * bbi[dirn][None] + qi_[..., None] * bbr[dirn][None]
        return jnp.concatenate([re, im], axis=2).transpose(1, 0, 3, 2).reshape(G, L * H, 2 * P)

    def cc(dirn, powers):
        wr = car[powers, dirn]
        wi = cai[powers, dirn]
        return jnp.concatenate([wr, -wi], axis=3).transpose(1, 3, 0, 2).reshape(G, 2 * P, L * H)

    ar_l = jnp.arange(L)
    def with_twin(m):
        return jnp.concatenate([m, m[..., P:], m[..., :P]], axis=2)

    w1 = jnp.concatenate([with_twin(bc(0, L - 1 - ar_l)), with_twin(bc(1, ar_l))], axis=2)
    w2 = jnp.concatenate([t_all, cc(0, ar_l + 1), cc(1, L - ar_l)], axis=1)
    coef = jnp.stack([jnp.concatenate([pr[L, 0], pr[L, 0]], -1), jnp.concatenate([-pi[L, 0], pi[L, 0]], -1),
                      jnp.concatenate([pr[L, 1], pr[L, 1]], -1), jnp.concatenate([-pi[L, 1], pi[L, 1]], -1)], axis=1)
    return w1.astype(BF16), w2.astype(BF16), coef


def _dwconv_body(cur_ref, prev_ref, next_ref, w_ref, b_ref, o_ref, zp_ref, *, tile, slab, seq_prompt, rows_sub):
    t = pl.program_id(0)
    row0 = t * tile
    seq_len = jnp.where(row0 < slab, seq_prompt, slab)
    pos = row0 % seq_len
    has_prev = pos != 0
    has_next = pos + tile != seq_len
    halo = CONV_HALO
    sub = 8
    rows = tile + 2 * halo
    zp_ref[0, 0:halo, :] = jnp.where(has_prev, prev_ref[...], 0.0)
    zp_ref[0, halo:halo + tile, :] = cur_ref[...]
    zp_ref[0, halo + tile:, :] = jnp.where(has_next, next_ref[...], 0.0)
    for j in range(1, sub):
        zp_ref[j, 0:rows - sub, :] = zp_ref[0, j:j + rows - sub, :]
    w = w_ref[...]
    off = halo - CONV_WIDTH // 2
    for r0 in range(0, tile, rows_sub):
        acc = jnp.zeros((rows_sub, w.shape[1]), F32) + b_ref[...]
        for kk in range(CONV_WIDTH):
            q, j = divmod(off + kk, sub)
            acc = acc + w[kk:kk + 1, :] * zp_ref[j, r0 + sub * q:r0 + sub * q + rows_sub, :]
        o_ref[r0:r0 + rows_sub, :] = acc


def dwconv(z, w, b, *, tile, cb, slab, seq_prompt, rows_sub=32):
    m, d = z.shape
    halo = CONV_HALO
    hb = tile // halo
    n_halo_blocks = m // halo
    assert seq_prompt % tile == 0 and slab % tile == 0 and tile % halo == 0
    return pl.pallas_call(
        functools.partial(_dwconv_body, tile=tile, slab=slab, seq_prompt=seq_prompt, rows_sub=rows_sub),
        grid=(m // tile, d // cb),
        in_specs=[pl.BlockSpec((tile, cb), lambda t, c: (t, c)),
                  pl.BlockSpec((halo, cb), lambda t, c: (jnp.maximum(t * hb - 1, 0), c)),
                  pl.BlockSpec((halo, cb), lambda t, c: (jnp.minimum((t + 1) * hb, n_halo_blocks - 1), c)),
                  pl.BlockSpec((CONV_WIDTH, cb), lambda t, c: (0, c)),
                  pl.BlockSpec((1, cb), lambda t, c: (0, c))],
        out_specs=pl.BlockSpec((tile, cb), lambda t, c: (t, c)),
        out_shape=jax.ShapeDtypeStruct((m, d), F32),
        scratch_shapes=[pltpu.VMEM((8, tile + 2 * halo, cb), F32)],
        compiler_params=_cparams(("parallel", "parallel")), name="dwconv")(z, z, z, w, b)


def _moe_body(be_ref, nu_ref, x_ref, w1_ref, w3_ref, w2_ref, o_ref, w1s, w3s, w2s):
    b = pl.program_id(0)
    prev = be_ref[jnp.maximum(b - 1, 0)]

    @pl.when((b == 0) | (be_ref[b] != prev))
    def _():
        w1s[...] = w1_ref[0].astype(BF16)
        w3s[...] = w3_ref[0].astype(BF16)
        w2s[...] = w2_ref[0].astype(BF16)

    @pl.when(b < nu_ref[0])
    def _():
        x = x_ref[...]
        a = jnp.dot(x, w1s[...], preferred_element_type=F32)
        g = jnp.dot(x, w3s[...], preferred_element_type=F32)
        o_ref[...] = jnp.dot((_silu(a) * g).astype(BF16), w2s[...],
                             preferred_element_type=F32).astype(o_ref.dtype)

    @pl.when(b >= nu_ref[0])
    def _():
        o_ref[...] = jnp.zeros(o_ref.shape, o_ref.dtype)


def moe_experts(xg, block_expert, n_used, w1, w3, w2, *, layer, tm):
    mp, d = xg.shape
    f = w1.shape[3]
    grid_spec = pltpu.PrefetchScalarGridSpec(
        num_scalar_prefetch=2, grid=(mp // tm,),
        in_specs=[pl.BlockSpec((tm, d), lambda b, be, nu: (b, 0)),
                  pl.BlockSpec((None, 1, d, f), lambda b, be, nu: (layer, be[b], 0, 0)),
                  pl.BlockSpec((None, 1, d, f), lambda b, be, nu: (layer, be[b], 0, 0)),
                  pl.BlockSpec((None, 1, f, d), lambda b, be, nu: (layer, be[b], 0, 0))],
        out_specs=pl.BlockSpec((tm, d), lambda b, be, nu: (b, 0)),
        scratch_shapes=[pltpu.VMEM((d, f), BF16), pltpu.VMEM((d, f), BF16), pltpu.VMEM((f, d), BF16)])
    return pl.pallas_call(
        _moe_body, grid_spec=grid_spec, out_shape=jax.ShapeDtypeStruct((mp, d), BF16),
        compiler_params=_cparams(("arbitrary",)), name="moe_experts")(block_expert, n_used, xg, w1, w3, w2)


def _combine_body(x_ref, y1_ref, y2_ref, g_ref, gm_ref, o_ref):
    g = g_ref[...]
    o_ref[...] = x_ref[...] + gm_ref[0] * (g[:, 0:1] * y1_ref[...].astype(F32)
                                           + g[:, 1:2] * y2_ref[...].astype(F32))


def moe_combine(x, y1, y2, gates, gate_mod, *, tm, slab):
    m, d = x.shape
    tiles_per_slab = slab // tm
    row = pl.BlockSpec((tm, d), lambda i: (i, 0))
    return pl.pallas_call(
        _combine_body, grid=(m // tm,),
        in_specs=[row, row, row, pl.BlockSpec((tm, LANES), lambda i: (i, 0)),
                  pl.BlockSpec((1, 1, d), lambda i: (i // tiles_per_slab, 0, 0))],
        out_specs=row, out_shape=jax.ShapeDtypeStruct((m, d), F32),
        compiler_params=_cparams(("parallel",)), name="moe_combine")(x, y1, y2, gates, gate_mod)


def moe_layer(x, norm_g, shift, scale, gate_mod, router_w, b_router, w1, w3, w2, *, layer, tm_rows, tm_moe, slab):
    t, d = x.shape
    n_exp = w1.shape[1]
    hb, idx, gate8, cnt = route(x, norm_g, shift, scale, router_w[0], router_w[1], b_router, tm=tm_rows, slab=slab)
    expert = idx[0:TOP_K].T
    rank = idx[TOP_K:2 * TOP_K].T
    gate = gate8[0:TOP_K].T
    counts = cnt[:, 0].astype(jnp.int32)
    padded = (counts + tm_moe - 1) // tm_moe * tm_moe
    pends = jnp.cumsum(padded)
    pstarts = pends - padded
    dest = (pstarts[expert] + rank).astype(jnp.int32).reshape(-1)
    n_assign = t * TOP_K
    n_blocks = (n_assign + n_exp * (tm_moe - 1) + tm_moe - 1) // tm_moe
    block_row0 = jnp.arange(n_blocks, dtype=jnp.int32) * tm_moe
    block_expert = jnp.minimum(jnp.sum((pends[None, :] <= block_row0[:, None]).astype(jnp.int32), axis=1),
                               n_exp - 1).astype(jnp.int32)
    n_used = (pends[-1] // tm_moe).astype(jnp.int32).reshape(1)
    tok = jnp.arange(n_assign, dtype=jnp.int32) // TOP_K
    src = jnp.zeros((n_blocks * tm_moe,), jnp.int32).at[dest].set(tok)
    xg = hb.at[src].get(mode="promise_in_bounds")
    yg = moe_experts(xg, block_expert, n_used, w1, w3, w2, layer=layer, tm=tm_moe)
    dest2 = dest.reshape(t, TOP_K)
    y1 = yg.at[dest2[:, 0]].get(mode="promise_in_bounds")
    y2 = yg.at[dest2[:, 1]].get(mode="promise_in_bounds")
    gates = jnp.pad(gate, ((0, 0), (0, LANES - TOP_K)))
    return moe_combine(x, y1, y2, gates, gate_mod, tm=tm_rows, slab=slab)


def _rope_tables(n_tokens):
    rows = n_tokens // GRID_W
    r = jnp.repeat(jnp.arange(rows), GRID_W).astype(F32)
    col = jnp.tile(jnp.arange(GRID_W), rows).astype(F32)
    n_freq = 64 // 4
    inv = ROPE_BASE ** (-jnp.arange(n_freq, dtype=F32) / n_freq)
    ang = jnp.concatenate([r[:, None] * inv, col[:, None] * inv], axis=-1)
    c, s = jnp.cos(ang), jnp.sin(ang)
    return jnp.concatenate([c, c, c, c], axis=1), jnp.concatenate([-s, s, -s, s], axis=1)


def _pick_tile(n, pref):
    t = pref
    while n % t:
        t //= 2
    return t


def kernel(x_prompt, x_sample, cache_mla_ckv, cache_mla_krope, cache_gqa_k, cache_gqa_v, state_s5, c, c_ctx, norm_mix_g, norm_ffn_g, ada_w, ada_b, final_norm_g, mla_wq_a, mla_q_norm, mla_wq_b, mla_wkv_a, mla_kv_norm, mla_wkv_b, mla_wo, gqa_wq, gqa_wk, gqa_wv, gqa_wo, gqa_sink, s5_lam_re, s5_lam_im, s5_b_re, s5_b_im, s5_c_re, s5_c_im, s5_log_dt, s5_d, s5_w_glu, s5_b_glu, conv_w_pw1, conv_b_pw1, conv_w_dw, conv_b_dw, conv_ln_g, conv_ln_b, conv_w_pw2, conv_b_pw2, moe_w_router, moe_b_router, moe_w1, moe_w3, moe_w2):
    bp, sp, d = x_prompt.shape
    bs, ns, _ = x_sample.shape
    depth = ada_w.shape[0]
    past = cache_mla_ckv.shape[2]
    slab = ns
    assert bp * sp == slab, "prompt rows must fill exactly one slab"
    n_slab = 1 + bs
    t = n_slab * slab
    tm = _pick_tile(slab, 1024)
    tmh = _pick_tile(slab, 512)
    tn = _pick_tile(d, 1024)
    tnh = _pick_tile(d, 512)
    x = jnp.concatenate([x_prompt.reshape(slab, d), x_sample.reshape(bs * slab, d)], axis=0)

    rows_pad = -(-n_slab // 16) * 16
    cond = jnp.concatenate([c_ctx[None], c, jnp.zeros((rows_pad - n_slab, d), F32)], axis=0)
    mod = modulation_all(cond, ada_w, ada_b, _pick_tile(6 * d, 1024))
    mod = mod[:, :n_slab].reshape(depth, n_slab, 6, 1, d).transpose(0, 2, 1, 3, 4)

    cos_t, sin_t = _rope_tables(ns)
    w_r = moe_w_router.T
    w_r_hi = w_r.astype(BF16)
    w_r_lo = (w_r - w_r_hi.astype(F32)).astype(BF16)
    n_mixers = 4
    outs = {}

    for layer in range(depth):
        kind, r = layer % n_mixers, layer // n_mixers
        sh_mix, sc_mix, gt_mix, sh_ffn, sc_ffn, gt_ffn = [mod[layer, q] for q in range(6)]
        g_mix = norm_mix_g[layer][None]
        if kind == 0:
            qrank = mla_wq_a.shape[2]
            kvrank = mla_kv_norm.shape[1]
            hds = MLA_HEADS
            wa = jnp.concatenate([mla_wq_a[r], mla_wkv_a[r],
                                  jnp.zeros((d, LANES - MLA_ROPE), F32)], axis=1).astype(BF16)
            na = wa.shape[1]
            q_norm, kv_norm = mla_q_norm[r][None], mla_kv_norm[r][None]

            def epi_a(accs, i, j, ec, em, er, et, qrank=qrank, kvrank=kvrank, tiles=slab // tmh):
                a = accs[0]
                qn = _rms(a[:, :qrank], ec[0])
                ckv = _rms(a[:, qrank:qrank + kvrank], ec[1])
                kr = a[:, qrank + kvrank:]
                kr = jnp.where(i >= tiles, _rope_rot(kr, et[0], et[1]), kr)
                return qn, ckv, kr

            qn, ckv, kr = fused_mm(
                name="mla_proj_a", m=t, k=d, n=na, tm=tmh, tn=na, slab=slab,
                xs=[(x, 0)], pro_consts=[g_mix], pro_mods=[sh_mix, sc_mix], prologue=_pro_adaln,
                ws=[(wa, 0)], epi_tabs=[cos_t, sin_t], epi_consts=[q_norm, kv_norm], epilogue=epi_a,
                outs=[(qrank, BF16, qrank, lambda j: 0), (kvrank, F32, kvrank, lambda j: 0),
                      (LANES, F32, LANES, lambda j: 0)])
            outs['ckv'] = ckv[:slab].reshape(bp, 1, sp, kvrank)
            outs['krope'] = kr[:slab, :MLA_ROPE].reshape(bp, 1, sp, MLA_ROPE)
            wqb = mla_wq_b[r].reshape(qrank, hds, MLA_NOPE + MLA_ROPE)
            wqb_n = wqb[:, :, :MLA_NOPE].reshape(qrank, hds * MLA_NOPE).astype(BF16)
            wqb_r = wqb[:, :, MLA_NOPE:].reshape(qrank, hds * MLA_ROPE).astype(BF16)
            qscale = (MLA_NOPE + MLA_ROPE) ** -0.5 * math.log2(math.e)

            def epi_qn(accs, i, j, ec, em, er, et, qscale=qscale):
                return (accs[0] * qscale,)

            (q_nope,) = fused_mm(
                name="mla_q_nope", m=t, k=qrank, n=hds * MLA_NOPE, tm=tm, tn=_pick_tile(hds * MLA_NOPE, 1024),
                slab=slab, xs=[(qn, 0)], prologue=_pro_cast, ws=[(wqb_n, 0)], epilogue=epi_qn,
                outs=[(hds * MLA_NOPE, BF16, _pick_tile(hds * MLA_NOPE, 1024), lambda j: j)])

            def epi_qr(accs, i, j, ec, em, er, et, tiles=slab // tm, qscale=qscale):
                a = accs[0]
                return (jnp.where(i >= tiles, _rope_rot(a, et[0], et[1]), a) * qscale,)

            (q_rope,) = fused_mm(
                name="mla_q_rope", m=t, k=qrank, n=hds * MLA_ROPE, tm=tm, tn=hds * MLA_ROPE,
                slab=slab, xs=[(qn, 0)], prologue=_pro_cast, ws=[(wqb_r, 0)], epilogue=epi_qr,
                epi_tabs=[cos_t, sin_t], outs=[(hds * MLA_ROPE, BF16, hds * MLA_ROPE, lambda j: 0)])
            ckv_s = ckv[slab:].reshape(bs, ns, kvrank)
            kr_s = kr[slab:, :MLA_ROPE].reshape(bs, ns, MLA_ROPE)
            ckv_all = jnp.concatenate(
                [jnp.concatenate([ckv_s, cache_mla_ckv[:, r]], axis=1).reshape(-1, kvrank), ckv[:slab]],
                axis=0).astype(BF16)
            kr_all = jnp.concatenate(
                [jnp.concatenate([kr_s, cache_mla_krope[:, r]], axis=1).reshape(-1, MLA_ROPE),
                 kr[:slab, :MLA_ROPE]], axis=0).astype(BF16)
            wkvb = mla_wkv_b[r].reshape(kvrank, hds, MLA_NOPE + MLA_V)
            wkvb_k = wkvb[:, :, :MLA_NOPE].reshape(kvrank, hds * MLA_NOPE).astype(BF16)
            wkvb_vt = wkvb[:, :, MLA_NOPE:].reshape(kvrank, hds * MLA_V).T.astype(BF16)
            tk_rows = ckv_all.shape[0]
            tm_kv = _pick_tile(tk_rows, 1024)
            (k_nope,) = fused_mm(
                name="mla_k_expand", m=tk_rows, k=kvrank, n=hds * MLA_NOPE, tm=tm_kv,
                tn=_pick_tile(hds * MLA_NOPE, 1024), slab=tm_kv, xs=[(ckv_all, 0)], prologue=_pro_cast,
                ws=[(wkvb_k, 0)], epilogue=_epi_plain,
                outs=[(hds * MLA_NOPE, BF16, _pick_tile(hds * MLA_NOPE, 1024), lambda j: j)])
            v_t = mm_nt(wkvb_vt, ckv_all, tm=_pick_tile(tk_rows, 512), name="mla_v_expand_t")
            k_prompt0 = bs * (ns + past)
            att = mla_attention(q_nope, q_rope, k_nope, v_t, kr_all, n_heads=hds, batch=bp, nq_rows=sp,
                                nk_rows=sp, q_row0=0, k_row0=k_prompt0, tq=sp, tk=sp)
            tqs = _pick_tile(ns, 512)
            tks = next(c for c in (1536, 1152, 1024, 768, 512, 384, 256, 128) if (ns + past) % c == 0)
            att = mla_attention(q_nope, q_rope, k_nope, v_t, kr_all, n_heads=hds, batch=bs, nq_rows=ns,
                                nk_rows=ns + past, q_row0=slab, k_row0=0, tq=tqs, tk=tks, prev_out=att)
            (x,) = fused_mm(
                name="mla_out_proj", m=t, k=hds * MLA_V, n=d, tm=tm, tn=tnh, slab=slab,
                xs=[(att, 0)], prologue=_pro_cast, ws=[(mla_wo[r].astype(BF16), 0)],
                epi_mods=[gt_mix], epi_rows=[x], epilogue=_epi_residual,
                outs=[(d, F32, tnh, lambda j: j)])
        elif kind == 1:
            hq, hkv, hd = GQA_HEADS, GQA_KV_HEADS, GQA_HEAD_DIM
            wqkv = jnp.concatenate([gqa_wq[r], gqa_wk[r], gqa_wv[r]], axis=1).astype(BF16)
            nqkv = wqkv.shape[1]
            tn_qkv = hkv * hd
            n_rope_tiles = (hq * hd + hkv * hd) // tn_qkv
            n_q_tiles = (hq * hd) // tn_qkv

            qscale = hd ** -0.5 * math.log2(math.e)

            def epi_qkv(accs, i, j, ec, em, er, et, tiles=slab // tmh, n_rope_tiles=n_rope_tiles,
                        n_q_tiles=n_q_tiles, qscale=qscale):
                a = accs[0]
                roped = jnp.where((i >= tiles) & (j < n_rope_tiles), _rope_rot(a, et[0], et[1]), a)
                return roped * jnp.where(j < n_q_tiles, qscale, 1.0), a

            qkv, kv_f32 = fused_mm(
                name="gqa_qkv_proj", m=t, k=d, n=nqkv, tm=tmh, tn=tn_qkv, slab=slab,
                xs=[(x, 0)], pro_consts=[g_mix], pro_mods=[sh_mix, sc_mix], prologue=_pro_adaln,
                ws=[(wqkv, 0)], epi_tabs=[cos_t, sin_t], epilogue=epi_qkv,
                outs=[(nqkv, BF16, tn_qkv, lambda j: j),
                      (2 * tn_qkv, F32, tn_qkv, functools.partial(lambda j, nq: jnp.maximum(j - nq, 0), nq=n_q_tiles))])
            outs['gqa_k'] = kv_f32[:slab, :hkv * hd].reshape(bp, 1, sp, hkv, hd)
            outs['gqa_v'] = kv_f32[:slab, hkv * hd:].reshape(bp, 1, sp, hkv, hd)
            att = gqa_attention(qkv, gqa_sink[r], batch=bp, n_rows=sp, q_row0=0, window=False)
            ctx_k = cache_gqa_k[:, r].reshape(bs, past, hkv * hd).astype(BF16)
            ctx_v = cache_gqa_v[:, r].reshape(bs, past, hkv * hd).astype(BF16)
            att = gqa_attention(qkv, gqa_sink[r], batch=bs, n_rows=ns, q_row0=slab, window=True,
                                ctx_k=ctx_k, ctx_v=ctx_v, prev_out=att)
            (x,) = fused_mm(
                name="gqa_out_proj", m=t, k=hq * hd, n=d, tm=tm, tn=tnh, slab=slab,
                xs=[(att, 0)], prologue=_pro_cast, ws=[(gqa_wo[r].astype(BF16), 0)],
                epi_mods=[gt_mix], epi_rows=[x], epilogue=_epi_residual,
                outs=[(d, F32, tnh, lambda j: j)])
        elif kind == 2:
            grp, hch, pst, lch = d // S5_GROUP_CH, S5_GROUP_CH, S5_STATE, S5_SCAN_CHUNK
            hf = norm_rows(x, g_mix, tm=tmh, slab=slab, out_dtype=F32, shift=sh_mix, scale=sc_mix, name="adaln_s5")
            w1, w2, coef = s5_weights(s5_lam_re[r], s5_lam_im[r], s5_b_re[r], s5_b_im[r],
                                      s5_c_re[r], s5_c_im[r], s5_log_dt[r])

            sel, unsel = _s5_selectors(lch, hch)
            cp, cs = sp // lch, ns // lch
            rs = _pick_tile(cs, 128)
            sb = math.gcd(bp, 8)
            u_p = s5_pack(hf, sel, row0=0, nbatch=bp, nseq=sp, rows_blk=cp, seqs_blk=sb)
            u_s = s5_pack(hf, sel, row0=slab, nbatch=bs, nseq=ns, rows_blk=rs)
            h0_p = jnp.zeros((grp, 2, bp, 2 * pst), F32)
            h0_s = state_s5[:, r].transpose(2, 1, 0, 4, 3).reshape(grp, 2, bs, 2 * pst)
            gb = 2 if grp % 2 == 0 else 1
            y_p, fin_p = s5_scan(u_p, w1, w2, coef, h0_p, gb=gb, nb=bp)
            y_s, _ = s5_scan(u_s, w1, w2, coef, h0_s, gb=gb, nb=bs)
            outs['s5'] = fin_p.reshape(grp, 2, bp, 2, pst).transpose(2, 1, 0, 4, 3)[:, None]
            y = s5_unpack(y_p, unsel, t=t, row0=0, rows_blk=cp, seqs_blk=sb)
            y = s5_unpack(y_s, unsel, t=t, row0=slab, rows_blk=rs, prev_out=y)
            wg = s5_w_glu[r].astype(BF16)
            bg = s5_b_glu[r][None]
            (x,) = fused_mm(
                name="s5_glu", m=t, k=d, n=d, tm=tmh, tn=tnh, slab=slab,
                xs=[(y, 0), (hf, 0)], pro_consts=[s5_d[r][None]], prologue=_pro_s5_post,
                ws=[(wg, 0), (wg, d // tnh)], epi_cols=[(bg, 0), (bg, d // tnh)],
                epi_mods=[gt_mix], epi_rows=[x], epilogue=_epi_glu_residual,
                outs=[(d, F32, tnh, lambda j: j)])
        else:
            w1c = conv_w_pw1[r].astype(BF16)
            b1c = conv_b_pw1[r][None]
            (z,) = fused_mm(
                name="conv_pw1_glu", m=t, k=d, n=d, tm=tmh, tn=tn, slab=slab,
                xs=[(x, 0)], pro_consts=[g_mix], pro_mods=[sh_mix, sc_mix], prologue=_pro_adaln,
                ws=[(w1c, 0), (w1c, d // tn)], epi_cols=[(b1c, 0), (b1c, d // tn)], epilogue=_epi_glu,
                outs=[(d, F32, tn, lambda j: j)])
            zc = dwconv(z, conv_w_dw[r], conv_b_dw[r][None], tile=sp, cb=_pick_tile(d, 512), slab=slab,
                        seq_prompt=sp)
            (x,) = fused_mm(
                name="conv_pw2", m=t, k=d, n=d, tm=tmh, tn=tn, slab=slab,
                xs=[(zc, 0)], pro_consts=[conv_ln_g[r][None], conv_ln_b[r][None]], prologue=_pro_ln_silu,
                ws=[(conv_w_pw2[r].astype(BF16), 0)], epi_cols=[(conv_b_pw2[r][None], 0)],
                epi_mods=[gt_mix], epi_rows=[x], epilogue=_epi_bias_residual,
                outs=[(d, F32, tn, lambda j: j)])
        x = moe_layer(x, norm_ffn_g[layer][None], sh_ffn, sc_ffn, gt_ffn, (w_r_hi, w_r_lo), moe_b_router,
                      moe_w1, moe_w3, moe_w2, layer=layer, tm_rows=tmh, tm_moe=256, slab=slab)

    y = norm_rows(x, final_norm_g[None], tm=tmh, slab=slab, out_dtype=F32, name="final_norm")
    return (y[:slab].reshape(bp, sp, d), y[slab:].reshape(bs, ns, d),
            outs['ckv'], outs['krope'], outs['gqa_k'], outs['gqa_v'], outs['s5'])
```
